```python
import math
import jax, jax.numpy as jnp
from jax import lax
import numpy as np

D_MODEL = 1024
BATCH = 8
SEQ = 4096
DEPTH = 1

MIX_WIDTH = D_MODEL
ATTN_WIDTH = MIX_WIDTH // 2
POOL_WIDTH = MIX_WIDTH - ATTN_WIDTH
HEAD_DIM = 64
N_HEADS = ATTN_WIDTH // HEAD_DIM
POOL_WINDOWS = (2, 4, 8, 16)
N_POOL_GROUPS = len(POOL_WINDOWS)
POOL_GROUP = POOL_WIDTH // N_POOL_GROUPS
D_FF = 4 * D_MODEL
PLE_DIM = 256
Q_BLOCK = 128
LN_EPS = 1e-5
RMS_EPS = 1e-6
DN_ALPHA = float((2 * DEPTH) ** 0.25)
DN_BETA = float((8 * DEPTH) ** -0.25)
PROJ_WIDTH = 3 * ATTN_WIDTH + POOL_WIDTH

kernel_name = "hymba_stickbreak_pool_deepnorm"


def layer_norm(x, g, b):
    xf = x.astype(jnp.float32)
    mu = jnp.mean(xf, axis=-1, keepdims=True)
    var = jnp.mean(jnp.square(xf - mu), axis=-1, keepdims=True)
    y = (xf - mu) * lax.rsqrt(var + LN_EPS)
    return (y * g.astype(jnp.float32) + b.astype(jnp.float32)).astype(x.dtype)


def stick_breaking_attention(q, k, v):
    B, H, S, Dh = q.shape
    n_blk = S // Q_BLOCK
    scale = 1.0 / math.sqrt(Dh)
    q_blocks = q.reshape(B, H, n_blk, Q_BLOCK, Dh).transpose(2, 0, 1, 3, 4)
    starts = jnp.arange(n_blk, dtype=jnp.int32) * Q_BLOCK
    k_pos = jnp.arange(S, dtype=jnp.int32)
    kf = k.astype(jnp.float32)
    vf = v.astype(jnp.float32)

    def one_block(args):
        q_blk, start = args
        z = jnp.einsum("bhqd,bhkd->bhqk", q_blk.astype(jnp.float32), kf) * scale
        q_pos = start + jnp.arange(Q_BLOCK, dtype=jnp.int32)
        mask = k_pos[None, :] < q_pos[:, None]
        log_not = jnp.where(mask, jax.nn.log_sigmoid(-z), 0.0)
        suffix = lax.cumsum(log_not, axis=3, reverse=True) - log_not
        weights = jnp.where(mask, jnp.exp(jax.nn.log_sigmoid(z) + suffix), 0.0)
        return jnp.einsum("bhqk,bhkd->bhqd", weights, vf)

    out = lax.map(one_block, (q_blocks, starts))
    return out.transpose(1, 2, 0, 3, 4).reshape(B, H, S, Dh)


def multiscale_pool(u, w_pool, pool_scale):
    B, S, _ = u.shape
    uf = u.astype(jnp.float32)
    csum = jnp.cumsum(uf, axis=1)
    pos = jnp.arange(S, dtype=jnp.int32)
    diffs = []
    for g, w in enumerate(POOL_WINDOWS):
        sl = slice(g * POOL_GROUP, (g + 1) * POOL_GROUP)
        cg = csum[..., sl]
        lag = jnp.pad(cg, ((0, 0), (w, 0), (0, 0)))[:, :S]
        count = jnp.minimum(pos + 1, w).astype(jnp.float32)[None, :, None]
        diffs.append((cg - lag) / count - uf[..., sl])
    d = jnp.stack(diffs, axis=2)
    y = jnp.einsum("bsgc,gcd->bsgd", d, w_pool.astype(jnp.float32))
    return y.reshape(B, S, POOL_WIDTH) * pool_scale.astype(jnp.float32)


def setup_inputs(seed: int = 0) -> dict:
    key = jax.random.key(seed)
    ks = jax.random.split(key, 24)
    nrm = lambda k, shape, s: jax.random.normal(k, shape, jnp.float32) * s
    L = DEPTH
    x = jax.random.normal(ks[0], (BATCH, SEQ, D_MODEL), jnp.float32)
    p = jax.random.normal(ks[1], (L, BATCH, SEQ, PLE_DIM), jnp.float32)
    emb_ln_g = 1.0 + nrm(ks[2], (D_MODEL,), 0.02)
    emb_ln_b = nrm(ks[3], (D_MODEL,), 0.02)
    col_scale = jnp.concatenate([
        jnp.ones((2 * ATTN_WIDTH,), jnp.float32),
        jnp.full((ATTN_WIDTH + POOL_WIDTH,), DN_BETA, jnp.float32)])
    w_in = nrm(ks[4], (L, D_MODEL, PROJ_WIDTH), D_MODEL ** -0.5) * col_scale
    attn_out_g = 1.0 + nrm(ks[5], (L, ATTN_WIDTH), 0.02)
    w_pool = nrm(ks[6], (L, N_POOL_GROUPS, POOL_GROUP, POOL_GROUP), POOL_GROUP ** -0.5 * DN_BETA)
    pool_scale = 1.0 + nrm(ks[7], (L, POOL_WIDTH), 0.02)
    w_out = nrm(ks[8], (L, MIX_WIDTH, D_MODEL), MIX_WIDTH ** -0.5 * DN_BETA)
    ln1_g = 1.0 + nrm(ks[9], (L, D_MODEL), 0.02)
    ln1_b = nrm(ks[10], (L, D_MODEL), 0.02)
    w_up = nrm(ks[11], (L, D_MODEL, D_FF), D_MODEL ** -0.5 * DN_BETA)
    w_down = nrm(ks[12], (L, D_FF, D_MODEL), D_FF ** -0.5 * DN_BETA)
    ln2_g = 1.0 + nrm(ks[13], (L, D_MODEL), 0.02)
    ln2_b = nrm(ks[14], (L, D_MODEL), 0.02)
    w_ple = nrm(ks[15], (L, PLE_DIM, D_MODEL), PLE_DIM ** -0.5 * DN_BETA)
    w_ple_gate = nrm(ks[16], (L, D_MODEL, D_MODEL), D_MODEL ** -0.5)
    ln3_g = 1.0 + nrm(ks[17], (L, D_MODEL), 0.02)
    ln3_b = nrm(ks[18], (L, D_MODEL), 0.02)
    return {"x": x, "p": p, "emb_ln_g": emb_ln_g, "emb_ln_b": emb_ln_b,
            "w_in": w_in, "attn_out_g": attn_out_g, "w_pool": w_pool, "pool_scale": pool_scale,
            "w_out": w_out, "ln1_g": ln1_g, "ln1_b": ln1_b, "w_up": w_up, "w_down": w_down,
            "ln2_g": ln2_g, "ln2_b": ln2_b, "w_ple": w_ple, "w_ple_gate": w_ple_gate,
            "ln3_g": ln3_g, "ln3_b": ln3_b}


def reference(x, p, emb_ln_g, emb_ln_b, w_in, attn_out_g, w_pool, pool_scale, w_out,
              ln1_g, ln1_b, w_up, w_down, ln2_g, ln2_b, w_ple, w_ple_gate, ln3_g, ln3_b):
    B, S, D = x.shape
    dt = x.dtype
    x = layer_norm(x, emb_ln_g, emb_ln_b)
    for i in range(DEPTH):
        proj = x @ w_in[i]
        q, k, v, u = jnp.split(proj, [ATTN_WIDTH, 2 * ATTN_WIDTH, 3 * ATTN_WIDTH], axis=-1)
        to_heads = lambda t: t.reshape(B, S, N_HEADS, HEAD_DIM).transpose(0, 2, 1, 3)
        o = stick_breaking_attention(to_heads(q), to_heads(k), to_heads(v))
        o = o * lax.rsqrt(jnp.mean(o * o, axis=-1, keepdims=True) + RMS_EPS)
        o = o.transpose(0, 2, 1, 3).reshape(B, S, ATTN_WIDTH) * attn_out_g[i].astype(jnp.float32)
        pooled = multiscale_pool(u, w_pool[i], pool_scale[i])
        mixed = jnp.concatenate([o, pooled], axis=-1).astype(dt) @ w_out[i]
        x = layer_norm(DN_ALPHA * x + mixed, ln1_g[i], ln1_b[i])
        h = jnp.square(jax.nn.relu(x @ w_up[i])) @ w_down[i]
        x = layer_norm(DN_ALPHA * x + h, ln2_g[i], ln2_b[i])
        ple = (p[i] @ w_ple[i]) * jax.nn.sigmoid(x @ w_ple_gate[i])
        x = layer_norm(DN_ALPHA * x + ple, ln3_g[i], ln3_b[i])
    return x
```

```python
import functools
import math

import jax
import jax.numpy as jnp
from jax import lax
from jax.experimental import pallas as pl
from jax.experimental.pallas import tpu as pltpu

HEAD_DIM = 64
POOL_WINDOWS = (2, 4, 8, 16)
LN_EPS = 1e-5
RMS_EPS = 1e-6

LANES = 128
HEADS_PER_LANE_TILE = LANES // HEAD_DIM
MAX_POOL_WINDOW = max(POOL_WINDOWS)
VMEM_LIMIT_BYTES = 56 * 1024 * 1024

TM_IN = 512
TQ = 256
TK = 256
TM_OUT = 256
FF_CHUNK = 1024

BF16 = jnp.bfloat16
F32 = jnp.float32


def _layer_norm(x, g, b):
    mu = jnp.mean(x, axis=-1, keepdims=True)
    xc = x - mu
    var = jnp.mean(xc * xc, axis=-1, keepdims=True)
    return xc * lax.rsqrt(var + LN_EPS) * g + b


def _const_spec(shape):
    zeros = (0,) * len(shape)
    return pl.BlockSpec(shape, lambda *_: zeros, pipeline_mode=pl.Buffered(1))


def _input_kernel(x_ref, g0_ref, b0_ref, w_in_ref, w_pool_ref, pscale_ref,
                  q_ref, k_ref, v_ref, pooled_ref, ubuf_ref, *, attn_width, pool_group):
    t = pl.program_id(1)
    tm = x_ref.shape[1]
    halo = MAX_POOL_WINDOW

    xn = _layer_norm(x_ref[0], g0_ref[...], b0_ref[...])
    proj = jnp.dot(xn.astype(BF16), w_in_ref[...], preferred_element_type=F32)

    a = attn_width
    q_ref[0] = (proj[:, :a] * (1.0 / math.sqrt(HEAD_DIM))).astype(BF16)
    k_ref[0] = proj[:, a:2 * a].astype(BF16)
    v_ref[0] = proj[:, 2 * a:3 * a].astype(BF16)
    u = proj[:, 3 * a:]

    @pl.when(t == 0)
    def _():
        ubuf_ref[0:halo, :] = jnp.zeros((halo, u.shape[1]), F32)

    @pl.when(t > 0)
    def _():
        ubuf_ref[0:halo, :] = ubuf_ref[tm:tm + halo, :]

    ubuf_ref[halo:halo + tm, :] = u

    pos = t * tm + lax.broadcasted_iota(jnp.int32, (tm, 1), 0)
    for g, w in enumerate(POOL_WINDOWS):
        lo, hi = g * pool_group, (g + 1) * pool_group
        win = ubuf_ref[halo:halo + tm, lo:hi]
        for i in range(1, w):
            win = win + ubuf_ref[halo - i:halo - i + tm, lo:hi]
        count = jnp.minimum(pos + 1, w).astype(F32)
        d = win / count - u[:, lo:hi]
        y = jnp.dot(d.astype(BF16), w_pool_ref[g], preferred_element_type=F32)
        pooled_ref[0, :, lo:hi] = (y * pscale_ref[:, lo:hi]).astype(BF16)


def _attn_kernel(q_ref, k_ref, v_ref, g_ref, o_ref):
    i = pl.program_id(2)
    tq = q_ref.shape[1]
    tk = TK
    nt = (((1,), (1,)), ((), ()))

    lane = lax.broadcasted_iota(jnp.int32, (1, LANES), 1)
    head_masks = [(lane >= h * HEAD_DIM) & (lane < (h + 1) * HEAD_DIM)
                  for h in range(HEADS_PER_LANE_TILE)]
    q = q_ref[0]
    q_heads = [jnp.where(m, q, jnp.zeros_like(q)) for m in head_masks]

    r = lax.broadcasted_iota(jnp.int32, (tk, tk), 0)
    c = lax.broadcasted_iota(jnp.int32, (tk, tk), 1)
    suffix_ones = (r >= c).astype(BF16)
    causal = c < r

    def block(qh, kb, vb, carry, acc, mask):
        z = lax.dot_general(qh, kb, nt, preferred_element_type=F32)
        sp = jnp.maximum(z, 0.0) + jnp.log(1.0 + jnp.exp(-jnp.abs(z)))
        if mask is not None:
            sp = jnp.where(mask, sp, 0.0)
        sp_hi = sp.astype(BF16)
        sp_lo = (sp - sp_hi.astype(F32)).astype(BF16)
        csum = (jnp.dot(sp_hi, suffix_ones, preferred_element_type=F32)
                + jnp.dot(sp_lo, suffix_ones, preferred_element_type=F32))
        w = jnp.exp(z - csum - carry)
        if mask is not None:
            w = jnp.where(mask, w, 0.0)
        acc = acc + jnp.dot(w.astype(BF16), vb, preferred_element_type=F32)
        carry = carry + jnp.sum(sp, axis=1, keepdims=True)
        return carry, acc

    nh = HEADS_PER_LANE_TILE
    zero_carry = jnp.zeros((tq, 1), F32)
    zero_acc = jnp.zeros((tq, LANES), F32)

    kd = k_ref[0, pl.ds(pl.multiple_of(i * tk, tk), tk), :]
    vd = v_ref[0, pl.ds(pl.multiple_of(i * tk, tk), tk), :]
    state = []
    for h in range(nh):
        state.extend(block(q_heads[h], kd, vd, zero_carry, zero_acc, causal))

    def body(n, st):
        j = i - 1 - n
        start = pl.multiple_of(j * tk, tk)
        kb = k_ref[0, pl.ds(start, tk), :]
        vb = v_ref[0, pl.ds(start, tk), :]
        out = []
        for h in range(nh):
            out.extend(block(q_heads[h], kb, vb, st[2 * h], st[2 * h + 1], None))
        return tuple(out)

    state = lax.fori_loop(0, i, body, tuple(state))

    o = jnp.zeros((tq, LANES), F32)
    inv = jnp.zeros((tq, LANES), F32)
    for h in range(nh):
        acc = state[2 * h + 1]
        sq = jnp.where(head_masks[h], acc * acc, 0.0)
        ms = jnp.sum(sq, axis=1, keepdims=True) * (1.0 / HEAD_DIM)
        o = jnp.where(head_masks[h], acc, o)
        inv = jnp.where(head_masks[h], lax.rsqrt(ms + RMS_EPS), inv)
    o_ref[0] = (o * inv * g_ref[...]).astype(BF16)


def _output_kernel(x_ref, o_ref, pooled_ref, p_ref,
                   g0_ref, b0_ref, w_out_ref, g1_ref, b1_ref,
                   w_up_ref, w_down_ref, g2_ref, b2_ref,
                   w_ple_ref, w_gate_ref, g3_ref, b3_ref,
                   out_ref, *, alpha):
    xn = _layer_norm(x_ref[0], g0_ref[...], b0_ref[...])
    mix_in = jnp.concatenate([o_ref[0], pooled_ref[0]], axis=-1)
    mixed = jnp.dot(mix_in, w_out_ref[...], preferred_element_type=F32)
    x1 = _layer_norm(alpha * xn + mixed, g1_ref[...], b1_ref[...])

    x1b = x1.astype(BF16)
    d_ff = w_up_ref.shape[1]
    y = jnp.zeros_like(x1)
    for c0 in range(0, d_ff, FF_CHUNK):
        h = jnp.dot(x1b, w_up_ref[:, c0:c0 + FF_CHUNK], preferred_element_type=F32)
        h = jnp.square(jnp.maximum(h, 0.0))
        y = y + jnp.dot(h.astype(BF16), w_down_ref[c0:c0 + FF_CHUNK, :], preferred_element_type=F32)
    x2 = _layer_norm(alpha * x1 + y, g2_ref[...], b2_ref[...])

    gate = jax.nn.sigmoid(jnp.dot(x2.astype(BF16), w_gate_ref[...], preferred_element_type=F32))
    ple = jnp.dot(p_ref[0].astype(BF16), w_ple_ref[...], preferred_element_type=F32) * gate
    out_ref[0] = _layer_norm(alpha * x2 + ple, g3_ref[...], b3_ref[...])


def _layer(x, p_i, g0, b0, w_in, attn_g, w_pool, pool_scale, w_out, ln1_g, ln1_b,
           w_up, w_down, ln2_g, ln2_b, w_ple, w_gate, ln3_g, ln3_b, alpha):
    B, S, D = x.shape
    n_groups, pool_group, _ = w_pool.shape
    pool_width = n_groups * pool_group
    attn_width = (w_in.shape[1] - pool_width) // 3
    mix_width = attn_width + pool_width
    d_ff = w_up.shape[1]
    ple_dim = p_i.shape[-1]
    assert S % TM_IN == 0 and S % TQ == 0 and S % TM_OUT == 0 and TQ == TK
    assert attn_width % LANES == 0 and d_ff % FF_CHUNK == 0 and TM_IN >= MAX_POOL_WINDOW

    row = lambda v: v.reshape(1, -1).astype(F32)
    cparams = functools.partial(pltpu.CompilerParams, vmem_limit_bytes=VMEM_LIMIT_BYTES)

    act = lambda width: jax.ShapeDtypeStruct((B, S, width), BF16)
    q, k, v, pooled = pl.pallas_call(
        functools.partial(_input_kernel, attn_width=attn_width, pool_group=pool_group),
        grid=(B, S // TM_IN),
        in_specs=[
            pl.BlockSpec((1, TM_IN, D), lambda b, t: (b, t, 0)),
            _const_spec((1, D)), _const_spec((1, D)),
            _const_spec(w_in.shape), _const_spec(w_pool.shape), _const_spec((1, pool_width)),
        ],
        out_specs=[
            pl.BlockSpec((1, TM_IN, attn_width), lambda b, t: (b, t, 0)),
            pl.BlockSpec((1, TM_IN, attn_width), lambda b, t: (b, t, 0)),
            pl.BlockSpec((1, TM_IN, attn_width), lambda b, t: (b, t, 0)),
            pl.BlockSpec((1, TM_IN, pool_width), lambda b, t: (b, t, 0)),
        ],
        out_shape=[act(attn_width), act(attn_width), act(attn_width), act(pool_width)],
        scratch_shapes=[pltpu.VMEM((TM_IN + MAX_POOL_WINDOW, pool_width), F32)],
        compiler_params=cparams(dimension_semantics=("parallel", "arbitrary")),
        name="hymba_input_stage",
    )(x, row(g0), row(b0), w_in.astype(BF16), w_pool.astype(BF16), row(pool_scale))

    n_pairs = attn_width // LANES
    o = pl.pallas_call(
        _attn_kernel,
        grid=(B, n_pairs, S // TQ),
        in_specs=[
            pl.BlockSpec((1, TQ, LANES), lambda b, hp, i: (b, i, hp)),
            pl.BlockSpec((1, S, LANES), lambda b, hp, i: (b, 0, hp)),
            pl.BlockSpec((1, S, LANES), lambda b, hp, i: (b, 0, hp)),
            pl.BlockSpec((1, LANES), lambda b, hp, i: (0, hp)),
        ],
        out_specs=pl.BlockSpec((1, TQ, LANES), lambda b, hp, i: (b, i, hp)),
        out_shape=act(attn_width),
        compiler_params=cparams(dimension_semantics=("parallel", "parallel", "arbitrary")),
        name="hymba_stickbreak_attention",
    )(q, k, v, row(attn_g))

    out = pl.pallas_call(
        functools.partial(_output_kernel, alpha=alpha),
        grid=(B, S // TM_OUT),
        in_specs=[
            pl.BlockSpec((1, TM_OUT, D), lambda b, t: (b, t, 0)),
            pl.BlockSpec((1, TM_OUT, attn_width), lambda b, t: (b, t, 0)),
            pl.BlockSpec((1, TM_OUT, pool_width), lambda b, t: (b, t, 0)),
            pl.BlockSpec((1, TM_OUT, ple_dim), lambda b, t: (b, t, 0)),
            _const_spec((1, D)), _const_spec((1, D)),
            _const_spec((mix_width, D)), _const_spec((1, D)), _const_spec((1, D)),
            _const_spec((D, d_ff)), _const_spec((d_ff, D)), _const_spec((1, D)), _const_spec((1, D)),
            _const_spec((ple_dim, D)), _const_spec((D, D)), _const_spec((1, D)), _const_spec((1, D)),
        ],
        out_specs=pl.BlockSpec((1, TM_OUT, D), lambda b, t: (b, t, 0)),
        out_shape=jax.ShapeDtypeStruct((B, S, D), x.dtype),
        compiler_params=cparams(dimension_semantics=("parallel", "parallel")),
        name="hymba_output_stage",
    )(x, o, pooled, p_i,
      row(g0), row(b0), w_out.astype(BF16), row(ln1_g), row(ln1_b),
      w_up.astype(BF16), w_down.astype(BF16), row(ln2_g), row(ln2_b),
      w_ple.astype(BF16), w_gate.astype(BF16), row(ln3_g), row(ln3_b))
    return out


def kernel(x, p, emb_ln_g, emb_ln_b, w_in, attn_out_g, w_pool, pool_scale, w_out, ln1_g, ln1_b, w_up, w_down, ln2_g, ln2_b, w_ple, w_ple_gate, ln3_g, ln3_b):
    depth = w_in.shape[0]
    assert depth == 1, "the fused output stage re-derives the layer input from x (single layer)"
    alpha = float((2 * depth) ** 0.25)
    return _layer(x, p[0], emb_ln_g, emb_ln_b, w_in[0], attn_out_g[0], w_pool[0], pool_scale[0],
                  w_out[0], ln1_g[0], ln1_b[0], w_up[0], w_down[0], ln2_g[0], ln2_b[0],
                  w_ple[0], w_ple_gate[0], ln3_g[0], ln3_b[0], alpha)
```

```python
import functools
import math

import jax
import jax.numpy as jnp
from jax import lax
from jax.experimental import pallas as pl
from jax.experimental.pallas import tpu as pltpu

HEAD_DIM = 64
POOL_WINDOWS = (2, 4, 8, 16)
LN_EPS = 1e-5
RMS_EPS = 1e-6
CARRY_DONE = 104.0

LANES = 128
HEADS_PER_LANE_TILE = LANES // HEAD_DIM
MAX_POOL_WINDOW = max(POOL_WINDOWS)
VMEM_LIMIT_BYTES = 56 * 1024 * 1024

TM_IN = 512
TQ = 256
TK = 256
TM_OUT = 256
FF_CHUNK = 1024

BF16 = jnp.bfloat16
F32 = jnp.float32


def _layer_norm(x, g, b):
    mu = jnp.mean(x, axis=-1, keepdims=True)
    xc = x - mu
    var = jnp.mean(xc * xc, axis=-1, keepdims=True)
    return xc * lax.rsqrt(var + LN_EPS) * g + b


def _const_spec(shape):
    zeros = (0,) * len(shape)
    return pl.BlockSpec(shape, lambda *_: zeros, pipeline_mode=pl.Buffered(1))


def _input_kernel(x_ref, g0_ref, b0_ref, w_in_ref, w_pool_ref, pscale_ref,
                  q_ref, k_ref, v_ref, pooled_ref, ubuf_ref, *, attn_width, pool_group):
    t = pl.program_id(1)
    tm = x_ref.shape[1]
    halo = MAX_POOL_WINDOW

    xn = _layer_norm(x_ref[0], g0_ref[...], b0_ref[...])
    proj = jnp.dot(xn.astype(BF16), w_in_ref[...], preferred_element_type=F32)

    a = attn_width
    q_ref[0] = (proj[:, :a] * (1.0 / math.sqrt(HEAD_DIM))).astype(BF16)
    k_ref[0] = proj[:, a:2 * a].astype(BF16)
    v_ref[0] = proj[:, 2 * a:3 * a].astype(BF16)
    u = proj[:, 3 * a:]

    @pl.when(t == 0)
    def _():
        ubuf_ref[0:halo, :] = jnp.zeros((halo, u.shape[1]), F32)

    @pl.when(t > 0)
    def _():
        ubuf_ref[0:halo, :] = ubuf_ref[tm:tm + halo, :]

    ubuf_ref[halo:halo + tm, :] = u

    pos = t * tm + lax.broadcasted_iota(jnp.int32, (tm, 1), 0)
    for g, w in enumerate(POOL_WINDOWS):
        lo, hi = g * pool_group, (g + 1) * pool_group
        win = ubuf_ref[halo:halo + tm, lo:hi]
        for i in range(1, w):
            win = win + ubuf_ref[halo - i:halo - i + tm, lo:hi]
        count = jnp.minimum(pos + 1, w).astype(F32)
        d = win / count - u[:, lo:hi]
        y = jnp.dot(d.astype(BF16), w_pool_ref[g], preferred_element_type=F32)
        pooled_ref[0, :, lo:hi] = (y * pscale_ref[:, lo:hi]).astype(BF16)


def _attn_kernel(q_ref, k_ref, v_ref, g_ref, o_ref):
    i = pl.program_id(2)
    tq = q_ref.shape[1]
    tk = TK
    nt = (((1,), (1,)), ((), ()))

    lane = lax.broadcasted_iota(jnp.int32, (1, LANES), 1)
    head_masks = [(lane >= h * HEAD_DIM) & (lane < (h + 1) * HEAD_DIM)
                  for h in range(HEADS_PER_LANE_TILE)]
    q = q_ref[0]
    q_heads = [jnp.where(m, q, jnp.zeros_like(q)) for m in head_masks]

    r = lax.broadcasted_iota(jnp.int32, (tk, tk), 0)
    c = lax.broadcasted_iota(jnp.int32, (tk, tk), 1)
    suffix_ones = (r >= c).astype(BF16)
    causal = c < r

    def block(qh, kb, vb, carry, acc, mask):
        z = lax.dot_general(qh, kb, nt, preferred_element_type=F32)
        sp = jnp.maximum(z, 0.0) + jnp.log(1.0 + jnp.exp(-jnp.abs(z)))
        if mask is not None:
            sp = jnp.where(mask, sp, 0.0)
        sp_hi = sp.astype(BF16)
        sp_lo = (sp - sp_hi.astype(F32)).astype(BF16)
        csum = (jnp.dot(sp_hi, suffix_ones, preferred_element_type=F32)
                + jnp.dot(sp_lo, suffix_ones, preferred_element_type=F32))
        w = jnp.exp(z - csum - carry)
        if mask is not None:
            w = jnp.where(mask, w, 0.0)
        acc = acc + jnp.dot(w.astype(BF16), vb, preferred_element_type=F32)
        carry = carry + jnp.sum(sp, axis=1, keepdims=True)
        return carry, acc

    nh = HEADS_PER_LANE_TILE
    zero_carry = jnp.zeros((tq, 1), F32)
    zero_acc = jnp.zeros((tq, LANES), F32)

    kd = k_ref[0, pl.ds(pl.multiple_of(i * tk, tk), tk), :]
    vd = v_ref[0, pl.ds(pl.multiple_of(i * tk, tk), tk), :]
    state = []
    for h in range(nh):
        state.extend(block(q_heads[h], kd, vd, zero_carry, zero_acc, causal))

    def carry_min(st):
        m = st[0]
        for h in range(1, nh):
            m = jnp.minimum(m, st[2 * h])
        return jnp.min(m)

    def cond(loop):
        n, cmin = loop[0], loop[1]
        return (n < i) & (cmin < CARRY_DONE)

    def body(loop):
        n, st = loop[0], loop[2:]
        j = i - 1 - n
        start = pl.multiple_of(j * tk, tk)
        kb = k_ref[0, pl.ds(start, tk), :]
        vb = v_ref[0, pl.ds(start, tk), :]
        out = []
        for h in range(nh):
            out.extend(block(q_heads[h], kb, vb, st[2 * h], st[2 * h + 1], None))
        return (n + 1, carry_min(out)) + tuple(out)

    state = lax.while_loop(cond, body, (jnp.int32(0), carry_min(state)) + tuple(state))[2:]

    o = jnp.zeros((tq, LANES), F32)
    inv = jnp.zeros((tq, LANES), F32)
    for h in range(nh):
        acc = state[2 * h + 1]
        sq = jnp.where(head_masks[h], acc * acc, 0.0)
        ms = jnp.sum(sq, axis=1, keepdims=True) * (1.0 / HEAD_DIM)
        o = jnp.where(head_masks[h], acc, o)
        inv = jnp.where(head_masks[h], lax.rsqrt(ms + RMS_EPS), inv)
    o_ref[0] = (o * inv * g_ref[...]).astype(BF16)


def _output_kernel(x_ref, o_ref, pooled_ref, p_ref,
                   g0_ref, b0_ref, w_out_ref, g1_ref, b1_ref,
                   w_up_ref, w_down_ref, g2_ref, b2_ref,
                   w_ple_ref, w_gate_ref, g3_ref, b3_ref,
                   out_ref, *, alpha):
    xn = _layer_norm(x_ref[0], g0_ref[...], b0_ref[...])
    mix_in = jnp.concatenate([o_ref[0], pooled_ref[0]], axis=-1)
    mixed = jnp.dot(mix_in, w_out_ref[...], preferred_element_type=F32)
    x1 = _layer_norm(alpha * xn + mixed, g1_ref[...], b1_ref[...])

    x1b = x1.astype(BF16)
    d_ff = w_up_ref.shape[1]
    y = jnp.zeros_like(x1)
    for c0 in range(0, d_ff, FF_CHUNK):
        h = jnp.dot(x1b, w_up_ref[:, c0:c0 + FF_CHUNK], preferred_element_type=F32)
        h = jnp.square(jnp.maximum(h, 0.0))
        y = y + jnp.dot(h.astype(BF16), w_down_ref[c0:c0 + FF_CHUNK, :], preferred_element_type=F32)
    x2 = _layer_norm(alpha * x1 + y, g2_ref[...], b2_ref[...])

    gate = jax.nn.sigmoid(jnp.dot(x2.astype(BF16), w_gate_ref[...], preferred_element_type=F32))
    ple = jnp.dot(p_ref[0].astype(BF16), w_ple_ref[...], preferred_element_type=F32) * gate
    out_ref[0] = _layer_norm(alpha * x2 + ple, g3_ref[...], b3_ref[...])


def _layer(x, p_i, g0, b0, w_in, attn_g, w_pool, pool_scale, w_out, ln1_g, ln1_b,
           w_up, w_down, ln2_g, ln2_b, w_ple, w_gate, ln3_g, ln3_b, alpha):
    B, S, D = x.shape
    n_groups, pool_group, _ = w_pool.shape
    pool_width = n_groups * pool_group
    attn_width = (w_in.shape[1] - pool_width) // 3
    mix_width = attn_width + pool_width
    d_ff = w_up.shape[1]
    ple_dim = p_i.shape[-1]
    assert S % TM_IN == 0 and S % TQ == 0 and S % TM_OUT == 0 and TQ == TK
    assert attn_width % LANES == 0 and d_ff % FF_CHUNK == 0 and TM_IN >= MAX_POOL_WINDOW

    row = lambda v: v.reshape(1, -1).astype(F32)
    cparams = functools.partial(pltpu.CompilerParams, vmem_limit_bytes=VMEM_LIMIT_BYTES)

    act = lambda width: jax.ShapeDtypeStruct((B, S, width), BF16)
    q, k, v, pooled = pl.pallas_call(
        functools.partial(_input_kernel, attn_width=attn_width, pool_group=pool_group),
        grid=(B, S // TM_IN),
        in_specs=[
            pl.BlockSpec((1, TM_IN, D), lambda b, t: (b, t, 0)),
            _const_spec((1, D)), _const_spec((1, D)),
            _const_spec(w_in.shape), _const_spec(w_pool.shape), _const_spec((1, pool_width)),
        ],
        out_specs=[
            pl.BlockSpec((1, TM_IN, attn_width), lambda b, t: (b, t, 0)),
            pl.BlockSpec((1, TM_IN, attn_width), lambda b, t: (b, t, 0)),
            pl.BlockSpec((1, TM_IN, attn_width), lambda b, t: (b, t, 0)),
            pl.BlockSpec((1, TM_IN, pool_width), lambda b, t: (b, t, 0)),
        ],
        out_shape=[act(attn_width), act(attn_width), act(attn_width), act(pool_width)],
        scratch_shapes=[pltpu.VMEM((TM_IN + MAX_POOL_WINDOW, pool_width), F32)],
        compiler_params=cparams(dimension_semantics=("parallel", "arbitrary")),
        name="hymba_input_stage",
    )(x, row(g0), row(b0), w_in.astype(BF16), w_pool.astype(BF16), row(pool_scale))

    n_pairs = attn_width // LANES
    o = pl.pallas_call(
        _attn_kernel,
        grid=(B, n_pairs, S // TQ),
        in_specs=[
            pl.BlockSpec((1, TQ, LANES), lambda b, hp, i: (b, i, hp)),
            pl.BlockSpec((1, S, LANES), lambda b, hp, i: (b, 0, hp)),
            pl.BlockSpec((1, S, LANES), lambda b, hp, i: (b, 0, hp)),
            pl.BlockSpec((1, LANES), lambda b, hp, i: (0, hp)),
        ],
        out_specs=pl.BlockSpec((1, TQ, LANES), lambda b, hp, i: (b, i, hp)),
        out_shape=act(attn_width),
        compiler_params=cparams(dimension_semantics=("parallel", "parallel", "arbitrary")),
        name="hymba_stickbreak_attention",
    )(q, k, v, row(attn_g))

    out = pl.pallas_call(
        functools.partial(_output_kernel, alpha=alpha),
        grid=(B, S // TM_OUT),
        in_specs=[
            pl.BlockSpec((1, TM_OUT, D), lambda b, t: (b, t, 0)),
            pl.BlockSpec((1, TM_OUT, attn_width), lambda b, t: (b, t, 0)),
            pl.BlockSpec((1, TM_OUT, pool_width), lambda b, t: (b, t, 0)),
            pl.BlockSpec((1, TM_OUT, ple_dim), lambda b, t: (b, t, 0)),
            _const_spec((1, D)), _const_spec((1, D)),
            _const_spec((mix_width, D)), _const_spec((1, D)), _const_spec((1, D)),
            _const_spec((D, d_ff)), _const_spec((d_ff, D)), _const_spec((1, D)), _const_spec((1, D)),
            _const_spec((ple_dim, D)), _const_spec((D, D)), _const_spec((1, D)), _const_spec((1, D)),
        ],
        out_specs=pl.BlockSpec((1, TM_OUT, D), lambda b, t: (b, t, 0)),
        out_shape=jax.ShapeDtypeStruct((B, S, D), x.dtype),
        compiler_params=cparams(dimension_semantics=("parallel", "parallel")),
        name="hymba_output_stage",
    )(x, o, pooled, p_i,
      row(g0), row(b0), w_out.astype(BF16), row(ln1_g), row(ln1_b),
      w_up.astype(BF16), w_down.astype(BF16), row(ln2_g), row(ln2_b),
      w_ple.astype(BF16), w_gate.astype(BF16), row(ln3_g), row(ln3_b))
    return out


def kernel(x, p, emb_ln_g, emb_ln_b, w_in, attn_out_g, w_pool, pool_scale, w_out, ln1_g, ln1_b, w_up, w_down, ln2_g, ln2_b, w_ple, w_ple_gate, ln3_g, ln3_b):
    depth = w_in.shape[0]
    assert depth == 1, "the fused output stage re-derives the layer input from x (single layer)"
    alpha = float((2 * depth) ** 0.25)
    return _layer(x, p[0], emb_ln_g, emb_ln_b, w_in[0], attn_out_g[0], w_pool[0], pool_scale[0],
                  w_out[0], ln1_g[0], ln1_b[0], w_up[0], w_down[0], ln2_g[0], ln2_b[0],
                  w_ple[0], w_ple_gate[0], ln3_g[0], ln3_b[0], alpha)
```

```python
import functools
import math

import jax
import jax.numpy as jnp
from jax import lax
from jax.experimental import pallas as pl
from jax.experimental.pallas import tpu as pltpu

HEAD_DIM = 64
POOL_WINDOWS = (2, 4, 8, 16)
LN_EPS = 1e-5
RMS_EPS = 1e-6
CARRY_DONE = 104.0
MASKED = 1e30

LANES = 128
MXU_DIM = 256
MAX_POOL_WINDOW = max(POOL_WINDOWS)
VMEM_LIMIT_BYTES = 56 * 1024 * 1024

TM_IN = 512
ATT_LANES = MXU_DIM
SUB_Q = ATT_LANES // (ATT_LANES // HEAD_DIM)
WIN = MXU_DIM
TQB = 512
SUBS_PER_STEP = 2
TM_OUT = 256
FF_CHUNK = 1024

BF16 = jnp.bfloat16
F32 = jnp.float32


def _layer_norm(x, g, b):
    mu = jnp.mean(x, axis=-1, keepdims=True)
    xc = x - mu
    var = jnp.mean(xc * xc, axis=-1, keepdims=True)
    return xc * lax.rsqrt(var + LN_EPS) * g + b


def _const_spec(shape):
    zeros = (0,) * len(shape)
    return pl.BlockSpec(shape, lambda *_: zeros, pipeline_mode=pl.Buffered(1))


def _input_kernel(x_ref, g0_ref, b0_ref, w_in_ref, w_pool_ref, pscale_ref,
                  q_ref, k_ref, v_ref, pooled_ref, ubuf_ref, *, attn_width, pool_group):
    t = pl.program_id(1)
    tm = x_ref.shape[1]
    halo = MAX_POOL_WINDOW

    xn = _layer_norm(x_ref[0], g0_ref[...], b0_ref[...])
    proj = jnp.dot(xn.astype(BF16), w_in_ref[...], preferred_element_type=F32)

    a = attn_width
    q_ref[0] = (proj[:, :a] * (1.0 / math.sqrt(HEAD_DIM))).astype(BF16)
    k_ref[0] = proj[:, a:2 * a].astype(BF16)
    v_ref[0] = proj[:, 2 * a:3 * a].astype(BF16)
    u = proj[:, 3 * a:]

    @pl.when(t == 0)
    def _():
        ubuf_ref[0:halo, :] = jnp.zeros((halo, u.shape[1]), F32)

    @pl.when(t > 0)
    def _():
        ubuf_ref[0:halo, :] = ubuf_ref[tm:tm + halo, :]

    ubuf_ref[halo:halo + tm, :] = u

    pos = t * tm + lax.broadcasted_iota(jnp.int32, (tm, 1), 0)
    for g, w in enumerate(POOL_WINDOWS):
        lo, hi = g * pool_group, (g + 1) * pool_group
        win = ubuf_ref[halo:halo + tm, lo:hi]
        for i in range(1, w):
            win = win + ubuf_ref[halo - i:halo - i + tm, lo:hi]
        count = jnp.minimum(pos + 1, w).astype(F32)
        d = win / count - u[:, lo:hi]
        y = jnp.dot(d.astype(BF16), w_pool_ref[g], preferred_element_type=F32)
        pooled_ref[0, :, lo:hi] = (y * pscale_ref[:, lo:hi]).astype(BF16)


def _attn_kernel(q_ref, k_ref, v_ref, g_ref, o_ref):
    qt = pl.program_id(2)
    tqb, lg = q_ref.shape[1], q_ref.shape[2]
    nh = lg // HEAD_DIM
    nt = (((1,), (1,)), ((), ()))
    tn = (((0,), (0,)), ((), ()))

    row_w = lax.broadcasted_iota(jnp.int32, (WIN, 1), 0)
    lane = lax.broadcasted_iota(jnp.int32, (1, lg), 1)
    lane_q = lane % SUB_Q
    lane_head = lane // HEAD_DIM
    r = lax.broadcasted_iota(jnp.int32, (WIN, WIN), 0)
    c = lax.broadcasted_iota(jnp.int32, (WIN, WIN), 1)
    suffix_ones = (c >= r).astype(BF16)
    same_head = (lax.broadcasted_iota(jnp.int32, (nh * SUB_Q, lg), 0) // SUB_Q) == lane_head
    head_masks = [lane_head == h for h in range(nh)]
    g = g_ref[...]

    def windows(subs):
        wss = [pl.multiple_of(jnp.maximum(hi - WIN, 0), SUB_Q) for hi, _, _, _ in subs]
        zs = [lax.dot_general(k_ref[0, pl.ds(ws, WIN), :], qbd_t, nt, preferred_element_type=F32)
              for ws, (_, _, qbd_t, _) in zip(wss, subs)]
        ts, splits = [], []
        for ws, z, (hi, q0, _, _) in zip(wss, zs, subs):
            valid = (ws + row_w) < jnp.minimum(q0 + lane_q, hi)
            t = jnp.where(valid, z, -MASKED)
            sp = jnp.maximum(t, 0.0) + jnp.log(1.0 + jnp.exp(-jnp.abs(t)))
            sp_hi = sp.astype(BF16)
            sp_lo = (sp - sp_hi.astype(F32)).astype(BF16)
            ts.append(t)
            splits.append((sp_hi, sp_lo))
        his = [jnp.dot(suffix_ones, sp_hi, preferred_element_type=F32) for sp_hi, _ in splits]
        los = [jnp.dot(suffix_ones, sp_lo, preferred_element_type=F32) for _, sp_lo in splits]
        csums = [a + b for a, b in zip(his, los)]
        ws_bf = [jnp.exp(t - csum - carry).astype(BF16)
                 for t, csum, (_, _, _, carry) in zip(ts, csums, subs)]
        ress = [lax.dot_general(w, v_ref[0, pl.ds(ws, WIN), :], tn, preferred_element_type=F32)
                for w, ws in zip(ws_bf, wss)]
        results = []
        for res, csum, ws, (_, _, _, carry) in zip(ress, csums, wss, subs):
            res = jnp.where(same_head, res, 0.0)
            out = res[0:SUB_Q]
            for h in range(1, nh):
                out = out + res[h * SUB_Q:(h + 1) * SUB_Q]
            results.append((out, carry + csum[0:1, :], ws))
        return results

    def finish(sl, o):
        inv = jnp.zeros_like(o)
        for m in head_masks:
            ms = jnp.sum(jnp.where(m, o * o, 0.0), axis=1, keepdims=True) * (1.0 / HEAD_DIM)
            inv = jnp.where(m, lax.rsqrt(ms + RMS_EPS), inv)
        o_ref[0, pl.ds(pl.multiple_of(sl * SUB_Q, SUB_Q), SUB_Q), :] = (o * inv * g).astype(BF16)

    def pair(p, _):
        subs = []
        for u in range(SUBS_PER_STEP):
            sl = p * SUBS_PER_STEP + u
            q0 = qt * tqb + sl * SUB_Q
            q_s = q_ref[0, pl.ds(pl.multiple_of(sl * SUB_Q, SUB_Q), SUB_Q), :]
            qbd_t = jnp.where(same_head, jnp.concatenate([q_s] * nh, axis=0), jnp.zeros((), BF16))
            subs.append((sl, q0, qbd_t))
        zero_carry = jnp.zeros((1, lg), F32)
        firsts = windows([(q0 + SUB_Q, q0, qbd_t, zero_carry) for _, q0, qbd_t in subs])

        for (sl, q0, qbd_t), (o, carry, ws) in zip(subs, firsts):
            def cond(st):
                return (st[0] > 0) & (st[1] < CARRY_DONE)

            def body(st, q0=q0, qbd_t=qbd_t):
                (o_new, carry_new, ws_new), = windows([(st[0], q0, qbd_t, st[2])])
                return ws_new, jnp.min(carry_new), carry_new, st[3] + o_new

            st = lax.while_loop(cond, body, (ws, jnp.min(carry), carry, o))
            finish(sl, st[3])
        return 0

    lax.fori_loop(0, tqb // (SUB_Q * SUBS_PER_STEP), pair, 0)


def _output_kernel(x_ref, o_ref, pooled_ref, p_ref,
                   g0_ref, b0_ref, w_out_ref, g1_ref, b1_ref,
                   w_up_ref, w_down_ref, g2_ref, b2_ref,
                   w_ple_ref, w_gate_ref, g3_ref, b3_ref,
                   out_ref, *, alpha):
    xn = _layer_norm(x_ref[0], g0_ref[...], b0_ref[...])
    mix_in = jnp.concatenate([o_ref[0], pooled_ref[0]], axis=-1)
    mixed = jnp.dot(mix_in, w_out_ref[...], preferred_element_type=F32)
    x1 = _layer_norm(alpha * xn + mixed, g1_ref[...], b1_ref[...])

    x1b = x1.astype(BF16)
    d_ff = w_up_ref.shape[1]
    y = jnp.zeros_like(x1)
    for c0 in range(0, d_ff, FF_CHUNK):
        h = jnp.dot(x1b, w_up_ref[:, c0:c0 + FF_CHUNK], preferred_element_type=F32)
        h = jnp.square(jnp.maximum(h, 0.0))
        y = y + jnp.dot(h.astype(BF16), w_down_ref[c0:c0 + FF_CHUNK, :], preferred_element_type=F32)
    x2 = _layer_norm(alpha * x1 + y, g2_ref[...], b2_ref[...])

    gate = jax.nn.sigmoid(jnp.dot(x2.astype(BF16), w_gate_ref[...], preferred_element_type=F32))
    ple = jnp.dot(p_ref[0].astype(BF16), w_ple_ref[...], preferred_element_type=F32) * gate
    out_ref[0] = _layer_norm(alpha * x2 + ple, g3_ref[...], b3_ref[...])


def _layer(x, p_i, g0, b0, w_in, attn_g, w_pool, pool_scale, w_out, ln1_g, ln1_b,
           w_up, w_down, ln2_g, ln2_b, w_ple, w_gate, ln3_g, ln3_b, alpha):
    B, S, D = x.shape
    n_groups, pool_group, _ = w_pool.shape
    pool_width = n_groups * pool_group
    attn_width = (w_in.shape[1] - pool_width) // 3
    mix_width = attn_width + pool_width
    d_ff = w_up.shape[1]
    ple_dim = p_i.shape[-1]
    assert S % TM_IN == 0 and S % TQB == 0 and S % TM_OUT == 0 and S >= WIN
    assert attn_width % ATT_LANES == 0 and TQB % (SUB_Q * SUBS_PER_STEP) == 0 and WIN % SUB_Q == 0
    assert d_ff % FF_CHUNK == 0 and TM_IN >= MAX_POOL_WINDOW

    row = lambda v: v.reshape(1, -1).astype(F32)
    cparams = functools.partial(pltpu.CompilerParams, vmem_limit_bytes=VMEM_LIMIT_BYTES)

    act = lambda width: jax.ShapeDtypeStruct((B, S, width), BF16)
    q, k, v, pooled = pl.pallas_call(
        functools.partial(_input_kernel, attn_width=attn_width, pool_group=pool_group),
        grid=(B, S // TM_IN),
        in_specs=[
            pl.BlockSpec((1, TM_IN, D), lambda b, t: (b, t, 0)),
            _const_spec((1, D)), _const_spec((1, D)),
            _const_spec(w_in.shape), _const_spec(w_pool.shape), _const_spec((1, pool_width)),
        ],
        out_specs=[
            pl.BlockSpec((1, TM_IN, attn_width), lambda b, t: (b, t, 0)),
            pl.BlockSpec((1, TM_IN, attn_width), lambda b, t: (b, t, 0)),
            pl.BlockSpec((1, TM_IN, attn_width), lambda b, t: (b, t, 0)),
            pl.BlockSpec((1, TM_IN, pool_width), lambda b, t: (b, t, 0)),
        ],
        out_shape=[act(attn_width), act(attn_width), act(attn_width), act(pool_width)],
        scratch_shapes=[pltpu.VMEM((TM_IN + MAX_POOL_WINDOW, pool_width), F32)],
        compiler_params=cparams(dimension_semantics=("parallel", "arbitrary")),
        name="hymba_input_stage",
    )(x, row(g0), row(b0), w_in.astype(BF16), w_pool.astype(BF16), row(pool_scale))

    n_groups_attn = attn_width // ATT_LANES
    o = pl.pallas_call(
        _attn_kernel,
        grid=(B, n_groups_attn, S // TQB),
        in_specs=[
            pl.BlockSpec((1, TQB, ATT_LANES), lambda b, hg, i: (b, i, hg)),
            pl.BlockSpec((1, S, ATT_LANES), lambda b, hg, i: (b, 0, hg)),
            pl.BlockSpec((1, S, ATT_LANES), lambda b, hg, i: (b, 0, hg)),
            pl.BlockSpec((1, ATT_LANES), lambda b, hg, i: (0, hg)),
        ],
        out_specs=pl.BlockSpec((1, TQB, ATT_LANES), lambda b, hg, i: (b, i, hg)),
        out_shape=act(attn_width),
        compiler_params=cparams(dimension_semantics=("parallel", "parallel", "arbitrary")),
        name="hymba_stickbreak_attention",
    )(q, k, v, row(attn_g))

    out = pl.pallas_call(
        functools.partial(_output_kernel, alpha=alpha),
        grid=(B, S // TM_OUT),
        in_specs=[
            pl.BlockSpec((1, TM_OUT, D), lambda b, t: (b, t, 0)),
            pl.BlockSpec((1, TM_OUT, attn_width), lambda b, t: (b, t, 0)),
            pl.BlockSpec((1, TM_OUT, pool_width), lambda b, t: (b, t, 0)),
            pl.BlockSpec((1, TM_OUT, ple_dim), lambda b, t: (b, t, 0)),
            _const_spec((1, D)), _const_spec((1, D)),
            _const_spec((mix_width, D)), _const_spec((1, D)), _const_spec((1, D)),
            _const_spec((D, d_ff)), _const_spec((d_ff, D)), _const_spec((1, D)), _const_spec((1, D)),
            _const_spec((ple_dim, D)), _const_spec((D, D)), _const_spec((1, D)), _const_spec((1, D)),
        ],
        out_specs=pl.BlockSpec((1, TM_OUT, D), lambda b, t: (b, t, 0)),
        out_shape=jax.ShapeDtypeStruct((B, S, D), x.dtype),
        compiler_params=cparams(dimension_semantics=("parallel", "parallel")),
        name="hymba_output_stage",
    )(x, o, pooled, p_i,
      row(g0), row(b0), w_out.astype(BF16), row(ln1_g), row(ln1_b),
      w_up.astype(BF16), w_down.astype(BF16), row(ln2_g), row(ln2_b),
      w_ple.astype(BF16), w_gate.astype(BF16), row(ln3_g), row(ln3_b))
    return out


def kernel(x, p, emb_ln_g, emb_ln_b, w_in, attn_out_g, w_pool, pool_scale, w_out, ln1_g, ln1_b, w_up, w_down, ln2_g, ln2_b, w_ple, w_ple_gate, ln3_g, ln3_b):
    depth = w_in.shape[0]
    assert depth == 1, "the fused output stage re-derives the layer input from x (single layer)"
    alpha = float((2 * depth) ** 0.25)
    return _layer(x, p[0], emb_ln_g, emb_ln_b, w_in[0], attn_out_g[0], w_pool[0], pool_scale[0],
                  w_out[0], ln1_g[0], ln1_b[0], w_up[0], w_down[0], ln2_g[0], ln2_b[0],
                  w_ple[0], w_ple_gate[0], ln3_g[0], ln3_b[0], alpha)
```

```python
import functools
import math

import jax
import jax.numpy as jnp
from jax import lax
from jax.experimental import pallas as pl
from jax.experimental.pallas import tpu as pltpu

HEAD_DIM = 64
POOL_WINDOWS = (2, 4, 8, 16)
LN_EPS = 1e-5
RMS_EPS = 1e-6
CARRY_DONE = 104.0
LOG2_E = 1.4426950408889634
MASKED = 1e30

LANES = 128
MXU_DIM = 256
MAX_POOL_WINDOW = max(POOL_WINDOWS)
VMEM_LIMIT_BYTES = 56 * 1024 * 1024

TM_IN = 512
ATT_LANES = MXU_DIM
SUB_Q = ATT_LANES // (ATT_LANES // HEAD_DIM)
WIN = MXU_DIM
SUBS_PER_STEP = 2
PIPE_DEPTH = 6
TM_OUT = 256
FF_CHUNK = 1024

BF16 = jnp.bfloat16
F32 = jnp.float32


def _layer_norm(x, g, b):
    mu = jnp.mean(x, axis=-1, keepdims=True)
    xc = x - mu
    var = jnp.mean(xc * xc, axis=-1, keepdims=True)
    return xc * lax.rsqrt(var + LN_EPS) * g + b


def _const_spec(shape):
    zeros = (0,) * len(shape)
    return pl.BlockSpec(shape, lambda *_: zeros, pipeline_mode=pl.Buffered(1))


def _input_kernel(x_ref, g0_ref, b0_ref, w_in_ref, w_pool_ref, pscale_ref,
                  q_ref, k_ref, v_ref, pooled_ref, ubuf_ref, *, attn_width, pool_group):
    t = pl.program_id(1)
    tm = x_ref.shape[1]
    halo = MAX_POOL_WINDOW

    xn = _layer_norm(x_ref[0], g0_ref[...], b0_ref[...])
    proj = jnp.dot(xn.astype(BF16), w_in_ref[...], preferred_element_type=F32)

    a = attn_width
    q_ref[0] = (proj[:, :a] * (1.0 / math.sqrt(HEAD_DIM))).astype(BF16)
    k_ref[0] = proj[:, a:2 * a].astype(BF16)
    v_ref[0] = proj[:, 2 * a:3 * a].astype(BF16)
    u = proj[:, 3 * a:]

    @pl.when(t == 0)
    def _():
        ubuf_ref[0:halo, :] = jnp.zeros((halo, u.shape[1]), F32)

    @pl.when(t > 0)
    def _():
        ubuf_ref[0:halo, :] = ubuf_ref[tm:tm + halo, :]

    ubuf_ref[halo:halo + tm, :] = u

    pos = t * tm + lax.broadcasted_iota(jnp.int32, (tm, 1), 0)
    for g, w in enumerate(POOL_WINDOWS):
        lo, hi = g * pool_group, (g + 1) * pool_group
        win = ubuf_ref[halo:halo + tm, lo:hi]
        for i in range(1, w):
            win = win + ubuf_ref[halo - i:halo - i + tm, lo:hi]
        count = jnp.minimum(pos + 1, w).astype(F32)
        d = win / count - u[:, lo:hi]
        y = jnp.dot(d.astype(BF16), w_pool_ref[g], preferred_element_type=F32)
        pooled_ref[0, :, lo:hi] = (y * pscale_ref[:, lo:hi]).astype(BF16)


def _attn_kernel(q_ref, k_ref, v_ref, g_ref, tri_ref, bd_ref, o_ref, osum_ref, carry_ref,
                 t_a, t_b, t_c, sp_buf, cs_buf, w_buf):
    s_len, lg = q_ref.shape[1], q_ref.shape[2]
    nh = lg // HEAD_DIM
    n_sub = s_len // SUB_Q
    n_steps = n_sub // SUBS_PER_STEP
    past = WIN - SUB_Q
    head_subs = past // SUB_Q + 1
    head_steps = -(-head_subs // SUBS_PER_STEP)
    first_sub = head_steps * SUBS_PER_STEP
    nt = (((1,), (1,)), ((), ()))
    tn = (((0,), (0,)), ((), ()))

    row_w = lax.broadcasted_iota(jnp.int32, (WIN, 1), 0)
    lane = lax.broadcasted_iota(jnp.int32, (1, lg), 1)
    lane_q = lane % SUB_Q
    lane_head = lane // HEAD_DIM
    diag_valid = lax.broadcasted_iota(jnp.int32, (SUB_Q, 1), 0) < lane_q
    head_masks = [lane_head == h for h in range(nh)]
    g = g_ref[...]

    def sub_rows(sl):
        return pl.ds(pl.multiple_of(sl * SUB_Q, SUB_Q), SUB_Q)

    def load_qbd(sl):
        q_s = q_ref[0, sub_rows(sl), :]
        return jnp.concatenate([q_s] * nh, axis=0) * bd_ref[...]

    def softplus(t):
        return jnp.maximum(t, 0.0) + jnp.log(1.0 + jnp.exp2(jnp.abs(t) * (-LOG2_E)))

    def head_blocks(res):
        out = res[0:SUB_Q]
        for h in range(1, nh):
            out = jnp.where(head_masks[h], res[h * SUB_Q:(h + 1) * SUB_Q], out)
        return out

    def finish(sl, o):
        inv = jnp.zeros_like(o)
        for m in head_masks:
            ms = jnp.sum(jnp.where(m, o * o, 0.0), axis=1, keepdims=True) * (1.0 / HEAD_DIM)
            inv = jnp.where(m, lax.rsqrt(ms + RMS_EPS), inv)
        o_ref[0, sub_rows(sl), :] = (o * inv * g).astype(BF16)

    def windows(subs):
        wss = [pl.multiple_of(jnp.maximum(hi - WIN, 0), SUB_Q) for hi, _, _, _ in subs]
        zs = [lax.dot_general(k_ref[0, pl.ds(ws, WIN), :], qbd_t, nt, preferred_element_type=F32)
              for ws, (_, _, qbd_t, _) in zip(wss, subs)]
        ts = [jnp.where((ws + row_w) < jnp.minimum(q0 + lane_q, hi), z, -MASKED)
              for ws, z, (hi, q0, _, _) in zip(wss, zs, subs)]
        csums = [jnp.dot(tri_ref[...], softplus(t).astype(BF16), preferred_element_type=F32) for t in ts]
        ws_bf = [jnp.exp(t - csum - carry).astype(BF16)
                 for t, csum, (_, _, _, carry) in zip(ts, csums, subs)]
        ress = [lax.dot_general(w, v_ref[0, pl.ds(ws, WIN), :], tn, preferred_element_type=F32)
                for w, ws in zip(ws_bf, wss)]
        return [(head_blocks(res), carry + csum[0:1, :], ws)
                for res, csum, ws, (_, _, _, carry) in zip(ress, csums, wss, subs)]

    zero_carry = jnp.zeros((1, lg), F32)
    for hs in range(head_steps):
        sls = [hs * SUBS_PER_STEP + u for u in range(SUBS_PER_STEP)]
        outs = windows([((sl + 1) * SUB_Q, sl * SUB_Q, load_qbd(sl), zero_carry) for sl in sls])
        for sl, (o, _, _) in zip(sls, outs):
            finish(sl, o)

    def step_subs(step):
        return [step * SUBS_PER_STEP + u for u in range(SUBS_PER_STEP)]

    def window_start(sl):
        return pl.multiple_of(sl * SUB_Q - past, SUB_Q)

    def stage_qk(step):
        for u, sl in enumerate(step_subs(step)):
            z = lax.dot_general(k_ref[0, pl.ds(window_start(sl), WIN), :], load_qbd(sl), nt,
                                preferred_element_type=F32)
            t_a[u] = jnp.concatenate([z[:past], jnp.where(diag_valid, z[past:], -MASKED)], axis=0)

    def stage_softplus():
        for u in range(SUBS_PER_STEP):
            t = t_a[u]
            sp_buf[u] = softplus(t).astype(BF16)
            t_b[u] = t

    def stage_csum():
        for u in range(SUBS_PER_STEP):
            cs_buf[u] = jnp.dot(tri_ref[...], sp_buf[u], preferred_element_type=F32)
            t_c[u] = t_b[u]

    def stage_weights(step):
        for u, sl in enumerate(step_subs(step)):
            cs = cs_buf[u]
            w_buf[u] = jnp.exp(t_c[u] - cs).astype(BF16)
            carry_ref[pl.ds(sl, 1), :] = cs[0:1, :]

    def stage_pv(step):
        for u, sl in enumerate(step_subs(step)):
            res = lax.dot_general(w_buf[u], v_ref[0, pl.ds(window_start(sl), WIN), :], tn,
                                  preferred_element_type=F32)
            osum_ref[sub_rows(sl), :] = head_blocks(res)

    def stage_norm(step):
        for sl in step_subs(step):
            finish(sl, osum_ref[sub_rows(sl), :])

    last_step = n_steps - 1
    for t_buf in (t_a, t_b, t_c):
        t_buf[...] = jnp.zeros(t_buf.shape, F32)
    sp_buf[...] = jnp.zeros(sp_buf.shape, BF16)
    cs_buf[...] = jnp.zeros(cs_buf.shape, F32)
    w_buf[...] = jnp.zeros(w_buf.shape, BF16)
    for sl in step_subs(head_steps):
        osum_ref[sub_rows(sl), :] = jnp.zeros((SUB_Q, lg), F32)

    def pipeline(i, _):
        on_grid = lambda step: jnp.clip(step, head_steps, last_step)
        stage_norm(on_grid(i - 5))
        stage_pv(on_grid(i - 4))
        stage_weights(on_grid(i - 3))
        stage_csum()
        stage_softplus()
        stage_qk(on_grid(i))
        return 0

    lax.fori_loop(head_steps, n_steps + PIPE_DEPTH - 1, pipeline, 0)

    @pl.when(jnp.min(carry_ref[first_sub:, :]) < CARRY_DONE)
    def _():
        def fix(sl, _):
            q0 = sl * SUB_Q
            qbd_t = load_qbd(sl)
            carry = carry_ref[pl.ds(sl, 1), :]

            def cond(st):
                return (st[0] > 0) & (st[1] < CARRY_DONE)

            def body(st):
                (o_new, carry_new, ws_new), = windows([(st[0], q0, qbd_t, st[2])])
                return ws_new, jnp.min(carry_new), carry_new, st[3] + o_new

            st = lax.while_loop(cond, body, (q0 - past, jnp.min(carry), carry, osum_ref[sub_rows(sl), :]))
            finish(sl, st[3])
            return 0

        lax.fori_loop(first_sub, n_sub, fix, 0)


def _output_kernel(x_ref, o_ref, pooled_ref, p_ref,
                   g0_ref, b0_ref, w_out_ref, g1_ref, b1_ref,
                   w_up_ref, w_down_ref, g2_ref, b2_ref,
                   w_ple_ref, w_gate_ref, g3_ref, b3_ref,
                   out_ref, *, alpha):
    xn = _layer_norm(x_ref[0], g0_ref[...], b0_ref[...])
    mix_in = jnp.concatenate([o_ref[0], pooled_ref[0]], axis=-1)
    mixed = jnp.dot(mix_in, w_out_ref[...], preferred_element_type=F32)
    x1 = _layer_norm(alpha * xn + mixed, g1_ref[...], b1_ref[...])

    x1b = x1.astype(BF16)
    d_ff = w_up_ref.shape[1]
    y = jnp.zeros_like(x1)
    for c0 in range(0, d_ff, FF_CHUNK):
        h = jnp.dot(x1b, w_up_ref[:, c0:c0 + FF_CHUNK], preferred_element_type=F32)
        h = jnp.square(jnp.maximum(h, 0.0))
        y = y + jnp.dot(h.astype(BF16), w_down_ref[c0:c0 + FF_CHUNK, :], preferred_element_type=F32)
    x2 = _layer_norm(alpha * x1 + y, g2_ref[...], b2_ref[...])

    gate = jax.nn.sigmoid(jnp.dot(x2.astype(BF16), w_gate_ref[...], preferred_element_type=F32))
    ple = jnp.dot(p_ref[0].astype(BF16), w_ple_ref[...], preferred_element_type=F32) * gate
    out_ref[0] = _layer_norm(alpha * x2 + ple, g3_ref[...], b3_ref[...])


def _layer(x, p_i, g0, b0, w_in, attn_g, w_pool, pool_scale, w_out, ln1_g, ln1_b,
           w_up, w_down, ln2_g, ln2_b, w_ple, w_gate, ln3_g, ln3_b, alpha):
    B, S, D = x.shape
    n_groups, pool_group, _ = w_pool.shape
    pool_width = n_groups * pool_group
    attn_width = (w_in.shape[1] - pool_width) // 3
    mix_width = attn_width + pool_width
    d_ff = w_up.shape[1]
    ple_dim = p_i.shape[-1]
    assert S % TM_IN == 0 and S % TM_OUT == 0 and S % (SUB_Q * SUBS_PER_STEP) == 0
    assert attn_width % ATT_LANES == 0 and WIN % SUB_Q == 0 and S >= 2 * WIN
    assert d_ff % FF_CHUNK == 0 and TM_IN >= MAX_POOL_WINDOW

    row = lambda v: v.reshape(1, -1).astype(F32)
    cparams = functools.partial(pltpu.CompilerParams, vmem_limit_bytes=VMEM_LIMIT_BYTES)

    act = lambda width: jax.ShapeDtypeStruct((B, S, width), BF16)
    q, k, v, pooled = pl.pallas_call(
        functools.partial(_input_kernel, attn_width=attn_width, pool_group=pool_group),
        grid=(B, S // TM_IN),
        in_specs=[
            pl.BlockSpec((1, TM_IN, D), lambda b, t: (b, t, 0)),
            _const_spec((1, D)), _const_spec((1, D)),
            _const_spec(w_in.shape), _const_spec(w_pool.shape), _const_spec((1, pool_width)),
        ],
        out_specs=[
            pl.BlockSpec((1, TM_IN, attn_width), lambda b, t: (b, t, 0)),
            pl.BlockSpec((1, TM_IN, attn_width), lambda b, t: (b, t, 0)),
            pl.BlockSpec((1, TM_IN, attn_width), lambda b, t: (b, t, 0)),
            pl.BlockSpec((1, TM_IN, pool_width), lambda b, t: (b, t, 0)),
        ],
        out_shape=[act(attn_width), act(attn_width), act(attn_width), act(pool_width)],
        scratch_shapes=[pltpu.VMEM((TM_IN + MAX_POOL_WINDOW, pool_width), F32)],
        compiler_params=cparams(dimension_semantics=("parallel", "arbitrary")),
        name="hymba_input_stage",
    )(x, row(g0), row(b0), w_in.astype(BF16), w_pool.astype(BF16), row(pool_scale))

    n_groups_attn = attn_width // ATT_LANES
    suffix_ones = jnp.triu(jnp.ones((WIN, WIN), BF16))
    head_of = jnp.arange(ATT_LANES, dtype=jnp.int32)
    block_diag = (head_of[:, None] // SUB_Q == head_of[None, :] // HEAD_DIM).astype(BF16)
    seq_spec = pl.BlockSpec((1, S, ATT_LANES), lambda b, hg: (b, 0, hg))
    o = pl.pallas_call(
        _attn_kernel,
        grid=(B, n_groups_attn),
        in_specs=[seq_spec, seq_spec, seq_spec, pl.BlockSpec((1, ATT_LANES), lambda b, hg: (0, hg)),
                  _const_spec((WIN, WIN)), _const_spec((ATT_LANES, ATT_LANES))],
        out_specs=seq_spec,
        out_shape=act(attn_width),
        scratch_shapes=[pltpu.VMEM((S, ATT_LANES), F32), pltpu.VMEM((S // SUB_Q, ATT_LANES), F32),
                        pltpu.VMEM((SUBS_PER_STEP, WIN, ATT_LANES), F32),
                        pltpu.VMEM((SUBS_PER_STEP, WIN, ATT_LANES), F32),
                        pltpu.VMEM((SUBS_PER_STEP, WIN, ATT_LANES), F32),
                        pltpu.VMEM((SUBS_PER_STEP, WIN, ATT_LANES), BF16),
                        pltpu.VMEM((SUBS_PER_STEP, WIN, ATT_LANES), F32),
                        pltpu.VMEM((SUBS_PER_STEP, WIN, ATT_LANES), BF16)],
        compiler_params=cparams(dimension_semantics=("parallel", "parallel")),
        name="hymba_stickbreak_attention",
    )(q, k, v, row(attn_g), suffix_ones, block_diag)

    out = pl.pallas_call(
        functools.partial(_output_kernel, alpha=alpha),
        grid=(B, S // TM_OUT),
        in_specs=[
            pl.BlockSpec((1, TM_OUT, D), lambda b, t: (b, t, 0)),
            pl.BlockSpec((1, TM_OUT, attn_width), lambda b, t: (b, t, 0)),
            pl.BlockSpec((1, TM_OUT, pool_width), lambda b, t: (b, t, 0)),
            pl.BlockSpec((1, TM_OUT, ple_dim), lambda b, t: (b, t, 0)),
            _const_spec((1, D)), _const_spec((1, D)),
            _const_spec((mix_width, D)), _const_spec((1, D)), _const_spec((1, D)),
            _const_spec((D, d_ff)), _const_spec((d_ff, D)), _const_spec((1, D)), _const_spec((1, D)),
            _const_spec((ple_dim, D)), _const_spec((D, D)), _const_spec((1, D)), _const_spec((1, D)),
        ],
        out_specs=pl.BlockSpec((1, TM_OUT, D), lambda b, t: (b, t, 0)),
        out_shape=jax.ShapeDtypeStruct((B, S, D), x.dtype),
        compiler_params=cparams(dimension_semantics=("parallel", "parallel")),
        name="hymba_output_stage",
    )(x, o, pooled, p_i,
      row(g0), row(b0), w_out.astype(BF16), row(ln1_g), row(ln1_b),
      w_up.astype(BF16), w_down.astype(BF16), row(ln2_g), row(ln2_b),
      w_ple.astype(BF16), w_gate.astype(BF16), row(ln3_g), row(ln3_b))
    return out


def kernel(x, p, emb_ln_g, emb_ln_b, w_in, attn_out_g, w_pool, pool_scale, w_out, ln1_g, ln1_b, w_up, w_down, ln2_g, ln2_b, w_ple, w_ple_gate, ln3_g, ln3_b):
    depth = w_in.shape[0]
    assert depth == 1, "the fused output stage re-derives the layer input from x (single layer)"
    alpha = float((2 * depth) ** 0.25)
    return _layer(x, p[0], emb_ln_g, emb_ln_b, w_in[0], attn_out_g[0], w_pool[0], pool_scale[0],
                  w_out[0], ln1_g[0], ln1_b[0], w_up[0], w_down[0], ln2_g[0], ln2_b[0],
                  w_ple[0], w_ple_gate[0], ln3_g[0], ln3_b[0], alpha)
```

```python
import functools
import math

import jax
import jax.numpy as jnp
from jax import lax
from jax.experimental import pallas as pl
from jax.experimental.pallas import tpu as pltpu

HEAD_DIM = 64
POOL_WINDOWS = (2, 4, 8, 16)
LN_EPS = 1e-5
RMS_EPS = 1e-6
CARRY_DONE = 104.0
LOG2_E = 1.4426950408889634
MASKED = 1e30

LANES = 128
MXU_DIM = 256
MAX_POOL_WINDOW = max(POOL_WINDOWS)
VMEM_LIMIT_BYTES = 56 * 1024 * 1024

TM_IN = 1024
ATT_LANES = MXU_DIM
SUB_Q = ATT_LANES // (ATT_LANES // HEAD_DIM)
WIN = MXU_DIM
SUBS_PER_STEP = 2
PIPE_DEPTH = 6
TM_OUT = 512
ROW_SUB = MXU_DIM
FF_CHUNK = 1024

BF16 = jnp.bfloat16
F32 = jnp.float32


def _layer_norm(x, g, b):
    mu = jnp.mean(x, axis=-1, keepdims=True)
    xc = x - mu
    var = jnp.mean(xc * xc, axis=-1, keepdims=True)
    return xc * lax.rsqrt(var + LN_EPS) * g + b


def _const_spec(shape):
    zeros = (0,) * len(shape)
    return pl.BlockSpec(shape, lambda *_: zeros, pipeline_mode=pl.Buffered(1))


def _input_kernel(x_ref, g0_ref, b0_ref, w_in_ref, w_pool_ref, pscale_ref,
                  q_ref, k_ref, v_ref, pooled_ref, ubuf_ref, *, attn_width, pool_group):
    t = pl.program_id(1)
    tm = x_ref.shape[1]
    halo = MAX_POOL_WINDOW

    @pl.when(t == 0)
    def _():
        ubuf_ref[0:halo, :] = jnp.zeros((halo, ubuf_ref.shape[1]), F32)

    @pl.when(t > 0)
    def _():
        ubuf_ref[0:halo, :] = ubuf_ref[tm:tm + halo, :]

    starts = list(range(0, tm, ROW_SUB))
    xn = [_layer_norm(x_ref[0, r0:r0 + ROW_SUB, :], g0_ref[...], b0_ref[...]) for r0 in starts]
    projs = [jnp.dot(a.astype(BF16), w_in_ref[...], preferred_element_type=F32) for a in xn]

    a = attn_width
    for r0, proj in zip(starts, projs):
        rows = slice(r0, r0 + ROW_SUB)
        q_ref[0, rows, :] = (proj[:, :a] * (1.0 / math.sqrt(HEAD_DIM))).astype(BF16)
        k_ref[0, rows, :] = proj[:, a:2 * a].astype(BF16)
        v_ref[0, rows, :] = proj[:, 2 * a:3 * a].astype(BF16)
        u = proj[:, 3 * a:]
        ubuf_ref[halo + r0:halo + r0 + ROW_SUB, :] = u

        pos = t * tm + r0 + lax.broadcasted_iota(jnp.int32, (ROW_SUB, 1), 0)
        for g, w in enumerate(POOL_WINDOWS):
            lo, hi = g * pool_group, (g + 1) * pool_group
            win = u[:, lo:hi]
            for i in range(1, w):
                win = win + ubuf_ref[halo + r0 - i:halo + r0 - i + ROW_SUB, lo:hi]
            count = jnp.minimum(pos + 1, w).astype(F32)
            d = win / count - u[:, lo:hi]
            y = jnp.dot(d.astype(BF16), w_pool_ref[g], preferred_element_type=F32)
            pooled_ref[0, rows, lo:hi] = (y * pscale_ref[:, lo:hi]).astype(BF16)


def _attn_kernel(q_ref, k_ref, v_ref, g_ref, tri_ref, bd_ref, o_ref, osum_ref, carry_ref,
                 t_a, t_b, t_c, sp_buf, cs_buf, w_buf):
    s_len, lg = q_ref.shape[1], q_ref.shape[2]
    nh = lg // HEAD_DIM
    n_sub = s_len // SUB_Q
    n_steps = n_sub // SUBS_PER_STEP
    past = WIN - SUB_Q
    head_subs = past // SUB_Q + 1
    head_steps = -(-head_subs // SUBS_PER_STEP)
    first_sub = head_steps * SUBS_PER_STEP
    nt = (((1,), (1,)), ((), ()))
    tn = (((0,), (0,)), ((), ()))

    row_w = lax.broadcasted_iota(jnp.int32, (WIN, 1), 0)
    lane = lax.broadcasted_iota(jnp.int32, (1, lg), 1)
    lane_q = lane % SUB_Q
    lane_head = lane // HEAD_DIM
    diag_valid = lax.broadcasted_iota(jnp.int32, (SUB_Q, 1), 0) < lane_q
    head_masks = [lane_head == h for h in range(nh)]
    g = g_ref[...]

    def sub_rows(sl):
        return pl.ds(pl.multiple_of(sl * SUB_Q, SUB_Q), SUB_Q)

    def load_qbd(sl):
        q_s = q_ref[0, sub_rows(sl), :]
        return jnp.concatenate([q_s] * nh, axis=0) * bd_ref[...]

    def softplus(t):
        return jnp.maximum(t, 0.0) + jnp.log(1.0 + jnp.exp2(jnp.abs(t) * (-LOG2_E)))

    def head_blocks(res):
        out = res[0:SUB_Q]
        for h in range(1, nh):
            out = jnp.where(head_masks[h], res[h * SUB_Q:(h + 1) * SUB_Q], out)
        return out

    def finish(sl, o):
        inv = jnp.zeros_like(o)
        for m in head_masks:
            ms = jnp.sum(jnp.where(m, o * o, 0.0), axis=1, keepdims=True) * (1.0 / HEAD_DIM)
            inv = jnp.where(m, lax.rsqrt(ms + RMS_EPS), inv)
        o_ref[0, sub_rows(sl), :] = (o * inv * g).astype(BF16)

    def windows(subs):
        wss = [pl.multiple_of(jnp.maximum(hi - WIN, 0), SUB_Q) for hi, _, _, _ in subs]
        zs = [lax.dot_general(k_ref[0, pl.ds(ws, WIN), :], qbd_t, nt, preferred_element_type=F32)
              for ws, (_, _, qbd_t, _) in zip(wss, subs)]
        ts = [jnp.where((ws + row_w) < jnp.minimum(q0 + lane_q, hi), z, -MASKED)
              for ws, z, (hi, q0, _, _) in zip(wss, zs, subs)]
        csums = [jnp.dot(tri_ref[...], softplus(t).astype(BF16), preferred_element_type=F32) for t in ts]
        ws_bf = [jnp.exp(t - csum - carry).astype(BF16)
                 for t, csum, (_, _, _, carry) in zip(ts, csums, subs)]
        ress = [lax.dot_general(w, v_ref[0, pl.ds(ws, WIN), :], tn, preferred_element_type=F32)
                for w, ws in zip(ws_bf, wss)]
        return [(head_blocks(res), carry + csum[0:1, :], ws)
                for res, csum, ws, (_, _, _, carry) in zip(ress, csums, wss, subs)]

    zero_carry = jnp.zeros((1, lg), F32)
    for hs in range(head_steps):
        sls = [hs * SUBS_PER_STEP + u for u in range(SUBS_PER_STEP)]
        outs = windows([((sl + 1) * SUB_Q, sl * SUB_Q, load_qbd(sl), zero_carry) for sl in sls])
        for sl, (o, _, _) in zip(sls, outs):
            finish(sl, o)

    def step_subs(step):
        return [step * SUBS_PER_STEP + u for u in range(SUBS_PER_STEP)]

    def window_start(sl):
        return pl.multiple_of(sl * SUB_Q - past, SUB_Q)

    def stage_qk(step):
        for u, sl in enumerate(step_subs(step)):
            z = lax.dot_general(k_ref[0, pl.ds(window_start(sl), WIN), :], load_qbd(sl), nt,
                                preferred_element_type=F32)
            t_a[u] = jnp.concatenate([z[:past], jnp.where(diag_valid, z[past:], -MASKED)], axis=0)

    def stage_softplus():
        for u in range(SUBS_PER_STEP):
            t = t_a[u]
            sp_buf[u] = softplus(t).astype(BF16)
            t_b[u] = t

    def stage_csum():
        for u in range(SUBS_PER_STEP):
            cs_buf[u] = jnp.dot(tri_ref[...], sp_buf[u], preferred_element_type=F32)
            t_c[u] = t_b[u]

    def stage_weights(step):
        for u, sl in enumerate(step_subs(step)):
            cs = cs_buf[u]
            w_buf[u] = jnp.exp(t_c[u] - cs).astype(BF16)
            carry_ref[pl.ds(sl, 1), :] = cs[0:1, :]

    def stage_pv(step):
        for u, sl in enumerate(step_subs(step)):
            res = lax.dot_general(w_buf[u], v_ref[0, pl.ds(window_start(sl), WIN), :], tn,
                                  preferred_element_type=F32)
            osum_ref[sub_rows(sl), :] = head_blocks(res)

    def stage_norm(step):
        for sl in step_subs(step):
            finish(sl, osum_ref[sub_rows(sl), :])

    last_step = n_steps - 1
    for t_buf in (t_a, t_b, t_c):
        t_buf[...] = jnp.zeros(t_buf.shape, F32)
    sp_buf[...] = jnp.zeros(sp_buf.shape, BF16)
    cs_buf[...] = jnp.zeros(cs_buf.shape, F32)
    w_buf[...] = jnp.zeros(w_buf.shape, BF16)
    for sl in step_subs(head_steps):
        osum_ref[sub_rows(sl), :] = jnp.zeros((SUB_Q, lg), F32)

    def pipeline(i, _):
        on_grid = lambda step: jnp.clip(step, head_steps, last_step)
        stage_norm(on_grid(i - 5))
        stage_pv(on_grid(i - 4))
        stage_weights(on_grid(i - 3))
        stage_csum()
        stage_softplus()
        stage_qk(on_grid(i))
        return 0

    lax.fori_loop(head_steps, n_steps + PIPE_DEPTH - 1, pipeline, 0)

    @pl.when(jnp.min(carry_ref[first_sub:, :]) < CARRY_DONE)
    def _():
        def fix(sl, _):
            q0 = sl * SUB_Q
            qbd_t = load_qbd(sl)
            carry = carry_ref[pl.ds(sl, 1), :]

            def cond(st):
                return (st[0] > 0) & (st[1] < CARRY_DONE)

            def body(st):
                (o_new, carry_new, ws_new), = windows([(st[0], q0, qbd_t, st[2])])
                return ws_new, jnp.min(carry_new), carry_new, st[3] + o_new

            st = lax.while_loop(cond, body, (q0 - past, jnp.min(carry), carry, osum_ref[sub_rows(sl), :]))
            finish(sl, st[3])
            return 0

        lax.fori_loop(first_sub, n_sub, fix, 0)


def _output_kernel(x_ref, o_ref, pooled_ref, p_ref,
                   g0_ref, b0_ref, w_out_ref, g1_ref, b1_ref,
                   w_up_ref, w_down_ref, g2_ref, b2_ref,
                   w_ple_ref, w_gate_ref, g3_ref, b3_ref,
                   out_ref, *, alpha):
    tm = x_ref.shape[1]
    subs = [pl.ds(r0, ROW_SUB) for r0 in range(0, tm, ROW_SUB)]
    dot = functools.partial(jnp.dot, preferred_element_type=F32)

    xn = [_layer_norm(x_ref[0, rows, :], g0_ref[...], b0_ref[...]) for rows in subs]
    mixed = [dot(jnp.concatenate([o_ref[0, rows, :], pooled_ref[0, rows, :]], axis=-1), w_out_ref[...])
             for rows in subs]
    x1 = [_layer_norm(alpha * a + m, g1_ref[...], b1_ref[...]) for a, m in zip(xn, mixed)]

    x1b = [a.astype(BF16) for a in x1]
    d_ff = w_up_ref.shape[1]
    y = [None] * len(subs)
    for c0 in range(0, d_ff, FF_CHUNK):
        h = [dot(a, w_up_ref[:, c0:c0 + FF_CHUNK]) for a in x1b]
        h = [jnp.square(jnp.maximum(a, 0.0)).astype(BF16) for a in h]
        part = [dot(a, w_down_ref[c0:c0 + FF_CHUNK, :]) for a in h]
        y = [b if a is None else a + b for a, b in zip(y, part)]
    x2 = [_layer_norm(alpha * a + b, g2_ref[...], b2_ref[...]) for a, b in zip(x1, y)]

    gate = [jax.nn.sigmoid(dot(a.astype(BF16), w_gate_ref[...])) for a in x2]
    ple = [dot(p_ref[0, rows, :].astype(BF16), w_ple_ref[...]) * gt for rows, gt in zip(subs, gate)]
    for rows, a, b in zip(subs, x2, ple):
        out_ref[0, rows, :] = _layer_norm(alpha * a + b, g3_ref[...], b3_ref[...])


def _layer(x, p_i, g0, b0, w_in, attn_g, w_pool, pool_scale, w_out, ln1_g, ln1_b,
           w_up, w_down, ln2_g, ln2_b, w_ple, w_gate, ln3_g, ln3_b, alpha):
    B, S, D = x.shape
    n_groups, pool_group, _ = w_pool.shape
    pool_width = n_groups * pool_group
    attn_width = (w_in.shape[1] - pool_width) // 3
    mix_width = attn_width + pool_width
    d_ff = w_up.shape[1]
    ple_dim = p_i.shape[-1]
    assert S % TM_IN == 0 and S % TM_OUT == 0 and S % (SUB_Q * SUBS_PER_STEP) == 0
    assert attn_width % ATT_LANES == 0 and WIN % SUB_Q == 0 and S >= 2 * WIN
    assert d_ff % FF_CHUNK == 0 and ROW_SUB >= MAX_POOL_WINDOW and TM_IN % ROW_SUB == 0 and TM_OUT % ROW_SUB == 0

    row = lambda v: v.reshape(1, -1).astype(F32)
    cparams = functools.partial(pltpu.CompilerParams, vmem_limit_bytes=VMEM_LIMIT_BYTES)

    act = lambda width: jax.ShapeDtypeStruct((B, S, width), BF16)
    q, k, v, pooled = pl.pallas_call(
        functools.partial(_input_kernel, attn_width=attn_width, pool_group=pool_group),
        grid=(B, S // TM_IN),
        in_specs=[
            pl.BlockSpec((1, TM_IN, D), lambda b, t: (b, t, 0)),
            _const_spec((1, D)), _const_spec((1, D)),
            _const_spec(w_in.shape), _const_spec(w_pool.shape), _const_spec((1, pool_width)),
        ],
        out_specs=[
            pl.BlockSpec((1, TM_IN, attn_width), lambda b, t: (b, t, 0)),
            pl.BlockSpec((1, TM_IN, attn_width), lambda b, t: (b, t, 0)),
            pl.BlockSpec((1, TM_IN, attn_width), lambda b, t: (b, t, 0)),
            pl.BlockSpec((1, TM_IN, pool_width), lambda b, t: (b, t, 0)),
        ],
        out_shape=[act(attn_width), act(attn_width), act(attn_width), act(pool_width)],
        scratch_shapes=[pltpu.VMEM((TM_IN + MAX_POOL_WINDOW, pool_width), F32)],
        compiler_params=cparams(dimension_semantics=("parallel", "arbitrary")),
        name="hymba_input_stage",
    )(x, row(g0), row(b0), w_in.astype(BF16), w_pool.astype(BF16), row(pool_scale))

    n_groups_attn = attn_width // ATT_LANES
    suffix_ones = jnp.triu(jnp.ones((WIN, WIN), BF16))
    head_of = jnp.arange(ATT_LANES, dtype=jnp.int32)
    block_diag = (head_of[:, None] // SUB_Q == head_of[None, :] // HEAD_DIM).astype(BF16)
    seq_spec = pl.BlockSpec((1, S, ATT_LANES), lambda b, hg: (b, 0, hg))
    o = pl.pallas_call(
        _attn_kernel,
        grid=(B, n_groups_attn),
        in_specs=[seq_spec, seq_spec, seq_spec, pl.BlockSpec((1, ATT_LANES), lambda b, hg: (0, hg)),
                  _const_spec((WIN, WIN)), _const_spec((ATT_LANES, ATT_LANES))],
        out_specs=seq_spec,
        out_shape=act(attn_width),
        scratch_shapes=[pltpu.VMEM((S, ATT_LANES), F32), pltpu.VMEM((S // SUB_Q, ATT_LANES), F32),
                        pltpu.VMEM((SUBS_PER_STEP, WIN, ATT_LANES), F32),
                        pltpu.VMEM((SUBS_PER_STEP, WIN, ATT_LANES), F32),
                        pltpu.VMEM((SUBS_PER_STEP, WIN, ATT_LANES), F32),
                        pltpu.VMEM((SUBS_PER_STEP, WIN, ATT_LANES), BF16),
                        pltpu.VMEM((SUBS_PER_STEP, WIN, ATT_LANES), F32),
                        pltpu.VMEM((SUBS_PER_STEP, WIN, ATT_LANES), BF16)],
        compiler_params=cparams(dimension_semantics=("parallel", "parallel")),
        name="hymba_stickbreak_attention",
    )(q, k, v, row(attn_g), suffix_ones, block_diag)

    out = pl.pallas_call(
        functools.partial(_output_kernel, alpha=alpha),
        grid=(B, S // TM_OUT),
        in_specs=[
            pl.BlockSpec((1, TM_OUT, D), lambda b, t: (b, t, 0)),
            pl.BlockSpec((1, TM_OUT, attn_width), lambda b, t: (b, t, 0)),
            pl.BlockSpec((1, TM_OUT, pool_width), lambda b, t: (b, t, 0)),
            pl.BlockSpec((1, TM_OUT, ple_dim), lambda b, t: (b, t, 0)),
            _const_spec((1, D)), _const_spec((1, D)),
            _const_spec((mix_width, D)), _const_spec((1, D)), _const_spec((1, D)),
            _const_spec((D, d_ff)), _const_spec((d_ff, D)), _const_spec((1, D)), _const_spec((1, D)),
            _const_spec((ple_dim, D)), _const_spec((D, D)), _const_spec((1, D)), _const_spec((1, D)),
        ],
        out_specs=pl.BlockSpec((1, TM_OUT, D), lambda b, t: (b, t, 0)),
        out_shape=jax.ShapeDtypeStruct((B, S, D), x.dtype),
        compiler_params=cparams(dimension_semantics=("parallel", "parallel")),
        name="hymba_output_stage",
    )(x, o, pooled, p_i,
      row(g0), row(b0), w_out.astype(BF16), row(ln1_g), row(ln1_b),
      w_up.astype(BF16), w_down.astype(BF16), row(ln2_g), row(ln2_b),
      w_ple.astype(BF16), w_gate.astype(BF16), row(ln3_g), row(ln3_b))
    return out


def kernel(x, p, emb_ln_g, emb_ln_b, w_in, attn_out_g, w_pool, pool_scale, w_out, ln1_g, ln1_b, w_up, w_down, ln2_g, ln2_b, w_ple, w_ple_gate, ln3_g, ln3_b):
    depth = w_in.shape[0]
    assert depth == 1, "the fused output stage re-derives the layer input from x (single layer)"
    alpha = float((2 * depth) ** 0.25)
    return _layer(x, p[0], emb_ln_g, emb_ln_b, w_in[0], attn_out_g[0], w_pool[0], pool_scale[0],
                  w_out[0], ln1_g[0], ln1_b[0], w_up[0], w_down[0], ln2_g[0], ln2_b[0],
                  w_ple[0], w_ple_gate[0], ln3_g[0], ln3_b[0], alpha)
```

```python
import functools
import math

import jax
import jax.numpy as jnp
from jax import lax
from jax.experimental import pallas as pl
from jax.experimental.pallas import tpu as pltpu

HEAD_DIM = 64
POOL_WINDOWS = (2, 4, 8, 16)
LN_EPS = 1e-5
RMS_EPS = 1e-6
CARRY_DONE = 104.0
LOG2_E = 1.4426950408889634
MASKED = 1e30

LANES = 128
MXU_DIM = 256
MAX_POOL_WINDOW = max(POOL_WINDOWS)
VMEM_LIMIT_BYTES = 56 * 1024 * 1024

TM_IN = 1024
ATT_LANES = MXU_DIM
SUB_Q = ATT_LANES // (ATT_LANES // HEAD_DIM)
WIN = MXU_DIM
SUBS_PER_STEP = 2
PIPE_DEPTH = 5
PIPE_UNROLL = 2
TM_OUT = 512
ROW_SUB = MXU_DIM
FF_CHUNK = 1024

BF16 = jnp.bfloat16
F32 = jnp.float32


def _layer_norm(x, g, b):
    mu = jnp.mean(x, axis=-1, keepdims=True)
    xc = x - mu
    var = jnp.mean(xc * xc, axis=-1, keepdims=True)
    return xc * lax.rsqrt(var + LN_EPS) * g + b


def _head_rms_norm(o, g):
    lane = lax.broadcasted_iota(jnp.int32, (1, LANES), 1)
    first = lane < HEAD_DIM
    out = []
    for c0 in range(0, o.shape[1], LANES):
        tile = o[:, c0:c0 + LANES]
        sq = tile * tile
        s_all = jnp.sum(sq, axis=1, keepdims=True)
        s_first = jnp.sum(jnp.where(first, sq, 0.0), axis=1, keepdims=True)
        ms = jnp.where(first, s_first, s_all - s_first) * (1.0 / HEAD_DIM)
        out.append((tile * lax.rsqrt(ms + RMS_EPS) * g[:, c0:c0 + LANES]).astype(BF16))
    return jnp.concatenate(out, axis=1)


def _const_spec(shape):
    zeros = (0,) * len(shape)
    return pl.BlockSpec(shape, lambda *_: zeros, pipeline_mode=pl.Buffered(1))


def _input_kernel(x_ref, g0_ref, b0_ref, w_in_ref, w_pool_ref, pscale_ref,
                  q_ref, k_ref, v_ref, pooled_ref, ubuf_ref, *, attn_width, pool_group):
    t = pl.program_id(1)
    tm = x_ref.shape[1]
    halo = MAX_POOL_WINDOW

    @pl.when(t == 0)
    def _():
        ubuf_ref[0:halo, :] = jnp.zeros((halo, ubuf_ref.shape[1]), F32)

    @pl.when(t > 0)
    def _():
        ubuf_ref[0:halo, :] = ubuf_ref[tm:tm + halo, :]

    starts = list(range(0, tm, ROW_SUB))
    xn = [_layer_norm(x_ref[0, r0:r0 + ROW_SUB, :], g0_ref[...], b0_ref[...]) for r0 in starts]
    projs = [jnp.dot(a.astype(BF16), w_in_ref[...], preferred_element_type=F32) for a in xn]

    a = attn_width
    for r0, proj in zip(starts, projs):
        rows = slice(r0, r0 + ROW_SUB)
        q_ref[0, rows, :] = (proj[:, :a] * (1.0 / math.sqrt(HEAD_DIM))).astype(BF16)
        k_ref[0, rows, :] = proj[:, a:2 * a].astype(BF16)
        v_ref[0, rows, :] = proj[:, 2 * a:3 * a].astype(BF16)
        u = proj[:, 3 * a:]
        ubuf_ref[halo + r0:halo + r0 + ROW_SUB, :] = u

        pos = t * tm + r0 + lax.broadcasted_iota(jnp.int32, (ROW_SUB, 1), 0)
        for g, w in enumerate(POOL_WINDOWS):
            lo, hi = g * pool_group, (g + 1) * pool_group
            win = u[:, lo:hi]
            for i in range(1, w):
                win = win + ubuf_ref[halo + r0 - i:halo + r0 - i + ROW_SUB, lo:hi]
            count = jnp.minimum(pos + 1, w).astype(F32)
            d = win / count - u[:, lo:hi]
            y = jnp.dot(d.astype(BF16), w_pool_ref[g], preferred_element_type=F32)
            pooled_ref[0, rows, lo:hi] = (y * pscale_ref[:, lo:hi]).astype(BF16)


def _attn_kernel(q_ref, k_ref, v_ref, tri_ref, bd_ref, o_ref, osum_ref, carry_ref,
                 t_a, t_b, t_c, sp_buf, cs_buf, w_buf):
    s_len, lg = q_ref.shape[1], q_ref.shape[2]
    nh = lg // HEAD_DIM
    n_sub = s_len // SUB_Q
    n_steps = n_sub // SUBS_PER_STEP
    past = WIN - SUB_Q
    head_subs = past // SUB_Q + 1
    head_steps = -(-head_subs // SUBS_PER_STEP)
    first_sub = head_steps * SUBS_PER_STEP
    nt = (((1,), (1,)), ((), ()))
    tn = (((0,), (0,)), ((), ()))

    row_w = lax.broadcasted_iota(jnp.int32, (WIN, 1), 0)
    lane = lax.broadcasted_iota(jnp.int32, (1, lg), 1)
    lane_q = lane % SUB_Q
    lane_head = lane // HEAD_DIM
    diag_valid = lax.broadcasted_iota(jnp.int32, (SUB_Q, 1), 0) < lane_q
    head_masks = [lane_head == h for h in range(nh)]

    def sub_rows(sl):
        return pl.ds(pl.multiple_of(sl * SUB_Q, SUB_Q), SUB_Q)

    def load_qbd(sl):
        q_s = q_ref[0, sub_rows(sl), :]
        return jnp.concatenate([q_s] * nh, axis=0) * bd_ref[...]

    def softplus(t):
        return jnp.maximum(t, 0.0) + jnp.log(1.0 + jnp.exp2(jnp.abs(t) * (-LOG2_E)))

    def head_blocks(res):
        out = res[0:SUB_Q]
        for h in range(1, nh):
            out = jnp.where(head_masks[h], res[h * SUB_Q:(h + 1) * SUB_Q], out)
        return out

    def store_out(sl, o):
        o_ref[0, sub_rows(sl), :] = o.astype(BF16)

    def windows(subs):
        wss = [pl.multiple_of(jnp.maximum(hi - WIN, 0), SUB_Q) for hi, _, _, _ in subs]
        zs = [lax.dot_general(k_ref[0, pl.ds(ws, WIN), :], qbd_t, nt, preferred_element_type=F32)
              for ws, (_, _, qbd_t, _) in zip(wss, subs)]
        ts = [jnp.where((ws + row_w) < jnp.minimum(q0 + lane_q, hi), z, -MASKED)
              for ws, z, (hi, q0, _, _) in zip(wss, zs, subs)]
        csums = [jnp.dot(tri_ref[...], softplus(t).astype(BF16), preferred_element_type=F32) for t in ts]
        ws_bf = [jnp.exp(t - csum - carry).astype(BF16)
                 for t, csum, (_, _, _, carry) in zip(ts, csums, subs)]
        ress = [lax.dot_general(w, v_ref[0, pl.ds(ws, WIN), :], tn, preferred_element_type=F32)
                for w, ws in zip(ws_bf, wss)]
        return [(head_blocks(res), carry + csum[0:1, :], ws)
                for res, csum, ws, (_, _, _, carry) in zip(ress, csums, wss, subs)]

    zero_carry = jnp.zeros((1, lg), F32)
    for hs in range(head_steps):
        sls = [hs * SUBS_PER_STEP + u for u in range(SUBS_PER_STEP)]
        outs = windows([((sl + 1) * SUB_Q, sl * SUB_Q, load_qbd(sl), zero_carry) for sl in sls])
        for sl, (o, _, _) in zip(sls, outs):
            store_out(sl, o)

    def step_subs(step):
        return [step * SUBS_PER_STEP + u for u in range(SUBS_PER_STEP)]

    def window_start(sl):
        return pl.multiple_of(sl * SUB_Q - past, SUB_Q)

    def stage_qk_dots(step):
        return [lax.dot_general(k_ref[0, pl.ds(window_start(sl), WIN), :], load_qbd(sl), nt,
                                preferred_element_type=F32) for sl in step_subs(step)]

    def stage_qk_store(zs):
        for u, z in enumerate(zs):
            t_a[u] = jnp.concatenate([z[:past], jnp.where(diag_valid, z[past:], -MASKED)], axis=0)

    def stage_softplus():
        for u in range(SUBS_PER_STEP):
            t = t_a[u]
            sp_buf[u] = softplus(t).astype(BF16)
            t_b[u] = t

    def stage_csum_dots():
        return [jnp.dot(tri_ref[...], sp_buf[u], preferred_element_type=F32) for u in range(SUBS_PER_STEP)]

    def stage_csum_store(csums):
        for u, cs in enumerate(csums):
            cs_buf[u] = cs
            t_c[u] = t_b[u]

    def stage_weights(step):
        for u, sl in enumerate(step_subs(step)):
            cs = cs_buf[u]
            w_buf[u] = jnp.exp(t_c[u] - cs).astype(BF16)
            carry_ref[pl.ds(sl, 1), :] = cs[0:1, :]

    def stage_pv(step):
        for u, sl in enumerate(step_subs(step)):
            res = lax.dot_general(w_buf[u], v_ref[0, pl.ds(window_start(sl), WIN), :], tn,
                                  preferred_element_type=F32)
            o = head_blocks(res)
            osum_ref[sub_rows(sl), :] = o
            store_out(sl, o)

    last_step = n_steps - 1
    for t_buf in (t_a, t_b, t_c):
        t_buf[...] = jnp.zeros(t_buf.shape, F32)
    sp_buf[...] = jnp.zeros(sp_buf.shape, BF16)
    cs_buf[...] = jnp.zeros(cs_buf.shape, F32)
    w_buf[...] = jnp.zeros(w_buf.shape, BF16)

    def pipeline(j, _):
        on_grid = lambda step: jnp.clip(step, head_steps, last_step)
        for r in range(PIPE_UNROLL):
            i = head_steps + j * PIPE_UNROLL + r
            csums = stage_csum_dots()
            stage_pv(on_grid(i - 4))
            zs = stage_qk_dots(on_grid(i))
            stage_weights(on_grid(i - 3))
            stage_csum_store(csums)
            stage_softplus()
            stage_qk_store(zs)
        return 0

    n_iters = n_steps - head_steps + PIPE_DEPTH - 1
    lax.fori_loop(0, -(-n_iters // PIPE_UNROLL), pipeline, 0)

    @pl.when(jnp.min(carry_ref[first_sub:, :]) < CARRY_DONE)
    def _():
        def fix(sl, _):
            q0 = sl * SUB_Q
            qbd_t = load_qbd(sl)
            carry = carry_ref[pl.ds(sl, 1), :]

            def cond(st):
                return (st[0] > 0) & (st[1] < CARRY_DONE)

            def body(st):
                (o_new, carry_new, ws_new), = windows([(st[0], q0, qbd_t, st[2])])
                return ws_new, jnp.min(carry_new), carry_new, st[3] + o_new

            st = lax.while_loop(cond, body, (q0 - past, jnp.min(carry), carry, osum_ref[sub_rows(sl), :]))
            store_out(sl, st[3])
            return 0

        lax.fori_loop(first_sub, n_sub, fix, 0)


def _output_kernel(x_ref, o_ref, pooled_ref, p_ref, ga_ref,
                   g0_ref, b0_ref, w_out_ref, g1_ref, b1_ref,
                   w_up_ref, w_down_ref, g2_ref, b2_ref,
                   w_ple_ref, w_gate_ref, g3_ref, b3_ref,
                   out_ref, *, alpha):
    tm = x_ref.shape[1]
    subs = [pl.ds(r0, ROW_SUB) for r0 in range(0, tm, ROW_SUB)]
    dot = functools.partial(jnp.dot, preferred_element_type=F32)

    xn = [_layer_norm(x_ref[0, rows, :], g0_ref[...], b0_ref[...]) for rows in subs]
    heads = [_head_rms_norm(o_ref[0, rows, :].astype(F32), ga_ref[...]) for rows in subs]
    mixed = [dot(jnp.concatenate([a, pooled_ref[0, rows, :]], axis=-1), w_out_ref[...])
             for a, rows in zip(heads, subs)]
    x1 = [_layer_norm(alpha * a + m, g1_ref[...], b1_ref[...]) for a, m in zip(xn, mixed)]

    x1b = [a.astype(BF16) for a in x1]
    d_ff = w_up_ref.shape[1]
    y = [None] * len(subs)
    for c0 in range(0, d_ff, FF_CHUNK):
        h = [dot(a, w_up_ref[:, c0:c0 + FF_CHUNK]) for a in x1b]
        h = [jnp.square(jnp.maximum(a, 0.0)).astype(BF16) for a in h]
        part = [dot(a, w_down_ref[c0:c0 + FF_CHUNK, :]) for a in h]
        y = [b if a is None else a + b for a, b in zip(y, part)]
    x2 = [_layer_norm(alpha * a + b, g2_ref[...], b2_ref[...]) for a, b in zip(x1, y)]

    gate = [jax.nn.sigmoid(dot(a.astype(BF16), w_gate_ref[...])) for a in x2]
    ple = [dot(p_ref[0, rows, :].astype(BF16), w_ple_ref[...]) * gt for rows, gt in zip(subs, gate)]
    for rows, a, b in zip(subs, x2, ple):
        out_ref[0, rows, :] = _layer_norm(alpha * a + b, g3_ref[...], b3_ref[...])


def _layer(x, p_i, g0, b0, w_in, attn_g, w_pool, pool_scale, w_out, ln1_g, ln1_b,
           w_up, w_down, ln2_g, ln2_b, w_ple, w_gate, ln3_g, ln3_b, alpha):
    B, S, D = x.shape
    n_groups, pool_group, _ = w_pool.shape
    pool_width = n_groups * pool_group
    attn_width = (w_in.shape[1] - pool_width) // 3
    mix_width = attn_width + pool_width
    d_ff = w_up.shape[1]
    ple_dim = p_i.shape[-1]
    assert S % TM_IN == 0 and S % TM_OUT == 0 and S % (SUB_Q * SUBS_PER_STEP) == 0
    assert attn_width % ATT_LANES == 0 and WIN % SUB_Q == 0 and S >= 2 * WIN
    assert d_ff % FF_CHUNK == 0 and ROW_SUB >= MAX_POOL_WINDOW and TM_IN % ROW_SUB == 0 and TM_OUT % ROW_SUB == 0

    row = lambda v: v.reshape(1, -1).astype(F32)
    cparams = functools.partial(pltpu.CompilerParams, vmem_limit_bytes=VMEM_LIMIT_BYTES)

    act = lambda width: jax.ShapeDtypeStruct((B, S, width), BF16)
    q, k, v, pooled = pl.pallas_call(
        functools.partial(_input_kernel, attn_width=attn_width, pool_group=pool_group),
        grid=(B, S // TM_IN),
        in_specs=[
            pl.BlockSpec((1, TM_IN, D), lambda b, t: (b, t, 0)),
            _const_spec((1, D)), _const_spec((1, D)),
            _const_spec(w_in.shape), _const_spec(w_pool.shape), _const_spec((1, pool_width)),
        ],
        out_specs=[
            pl.BlockSpec((1, TM_IN, attn_width), lambda b, t: (b, t, 0)),
            pl.BlockSpec((1, TM_IN, attn_width), lambda b, t: (b, t, 0)),
            pl.BlockSpec((1, TM_IN, attn_width), lambda b, t: (b, t, 0)),
            pl.BlockSpec((1, TM_IN, pool_width), lambda b, t: (b, t, 0)),
        ],
        out_shape=[act(attn_width), act(attn_width), act(attn_width), act(pool_width)],
        scratch_shapes=[pltpu.VMEM((TM_IN + MAX_POOL_WINDOW, pool_width), F32)],
        compiler_params=cparams(dimension_semantics=("parallel", "arbitrary")),
        name="hymba_input_stage",
    )(x, row(g0), row(b0), w_in.astype(BF16), w_pool.astype(BF16), row(pool_scale))

    n_groups_attn = attn_width // ATT_LANES
    suffix_ones = jnp.triu(jnp.ones((WIN, WIN), BF16))
    head_of = jnp.arange(ATT_LANES, dtype=jnp.int32)
    block_diag = (head_of[:, None] // SUB_Q == head_of[None, :] // HEAD_DIM).astype(BF16)
    seq_spec = pl.BlockSpec((1, S, ATT_LANES), lambda b, hg: (b, 0, hg))
    o = pl.pallas_call(
        _attn_kernel,
        grid=(B, n_groups_attn),
        in_specs=[seq_spec, seq_spec, seq_spec, _const_spec((WIN, WIN)), _const_spec((ATT_LANES, ATT_LANES))],
        out_specs=seq_spec,
        out_shape=act(attn_width),
        scratch_shapes=[pltpu.VMEM((S, ATT_LANES), F32), pltpu.VMEM((S // SUB_Q, ATT_LANES), F32),
                        pltpu.VMEM((SUBS_PER_STEP, WIN, ATT_LANES), F32),
                        pltpu.VMEM((SUBS_PER_STEP, WIN, ATT_LANES), F32),
                        pltpu.VMEM((SUBS_PER_STEP, WIN, ATT_LANES), F32),
                        pltpu.VMEM((SUBS_PER_STEP, WIN, ATT_LANES), BF16),
                        pltpu.VMEM((SUBS_PER_STEP, WIN, ATT_LANES), F32),
                        pltpu.VMEM((SUBS_PER_STEP, WIN, ATT_LANES), BF16)],
        compiler_params=cparams(dimension_semantics=("parallel", "parallel")),
        name="hymba_stickbreak_attention",
    )(q, k, v, suffix_ones, block_diag)

    out = pl.pallas_call(
        functools.partial(_output_kernel, alpha=alpha),
        grid=(B, S // TM_OUT),
        in_specs=[
            pl.BlockSpec((1, TM_OUT, D), lambda b, t: (b, t, 0)),
            pl.BlockSpec((1, TM_OUT, attn_width), lambda b, t: (b, t, 0)),
            pl.BlockSpec((1, TM_OUT, pool_width), lambda b, t: (b, t, 0)),
            pl.BlockSpec((1, TM_OUT, ple_dim), lambda b, t: (b, t, 0)),
            _const_spec((1, attn_width)),
            _const_spec((1, D)), _const_spec((1, D)),
            _const_spec((mix_width, D)), _const_spec((1, D)), _const_spec((1, D)),
            _const_spec((D, d_ff)), _const_spec((d_ff, D)), _const_spec((1, D)), _const_spec((1, D)),
            _const_spec((ple_dim, D)), _const_spec((D, D)), _const_spec((1, D)), _const_spec((1, D)),
        ],
        out_specs=pl.BlockSpec((1, TM_OUT, D), lambda b, t: (b, t, 0)),
        out_shape=jax.ShapeDtypeStruct((B, S, D), x.dtype),
        compiler_params=cparams(dimension_semantics=("parallel", "parallel")),
        name="hymba_output_stage",
    )(x, o, pooled, p_i, row(attn_g),
      row(g0), row(b0), w_out.astype(BF16), row(ln1_g), row(ln1_b),
      w_up.astype(BF16), w_down.astype(BF16), row(ln2_g), row(ln2_b),
      w_ple.astype(BF16), w_gate.astype(BF16), row(ln3_g), row(ln3_b))
    return out


def kernel(x, p, emb_ln_g, emb_ln_b, w_in, attn_out_g, w_pool, pool_scale, w_out, ln1_g, ln1_b, w_up, w_down, ln2_g, ln2_b, w_ple, w_ple_gate, ln3_g, ln3_b):
    depth = w_in.shape[0]
    assert depth == 1, "the fused output stage re-derives the layer input from x (single layer)"
    alpha = float((2 * depth) ** 0.25)
    return _layer(x, p[0], emb_ln_g, emb_ln_b, w_in[0], attn_out_g[0], w_pool[0], pool_scale[0],
                  w_out[0], ln1_g[0], ln1_b[0], w_up[0], w_down[0], ln2_g[0], ln2_b[0],
                  w_ple[0], w_ple_gate[0], ln3_g[0], ln3_b[0], alpha)
```

```python
import functools
import math

import jax
import jax.numpy as jnp
from jax import lax
from jax.experimental import pallas as pl
from jax.experimental.pallas import tpu as pltpu

HEAD_DIM = 64
POOL_WINDOWS = (2, 4, 8, 16)
LN_EPS = 1e-5
RMS_EPS = 1e-6
CARRY_DONE = 104.0
LOG2_E = 1.4426950408889634
MASKED = 1e30

LANES = 128
MXU_DIM = 256
MAX_POOL_WINDOW = max(POOL_WINDOWS)
VMEM_LIMIT_BYTES = 56 * 1024 * 1024

TM_IN = 1024
ATT_LANES = MXU_DIM
SUB_Q = ATT_LANES // (ATT_LANES // HEAD_DIM)
WIN = MXU_DIM
SUBS_PER_STEP = 2
PIPE_DEPTH = 5
PIPE_UNROLL = 5
TM_OUT = 512
ROW_SUB = MXU_DIM
FF_CHUNK = 1024

BF16 = jnp.bfloat16
F32 = jnp.float32


def _layer_norm(x, g, b):
    mu = jnp.mean(x, axis=-1, keepdims=True)
    xc = x - mu
    var = jnp.mean(xc * xc, axis=-1, keepdims=True)
    return xc * lax.rsqrt(var + LN_EPS) * g + b


def _head_rms_norm(o, g):
    lane = lax.broadcasted_iota(jnp.int32, (1, LANES), 1)
    first = lane < HEAD_DIM
    out = []
    for c0 in range(0, o.shape[1], LANES):
        tile = o[:, c0:c0 + LANES]
        sq = tile * tile
        s_all = jnp.sum(sq, axis=1, keepdims=True)
        s_first = jnp.sum(jnp.where(first, sq, 0.0), axis=1, keepdims=True)
        ms = jnp.where(first, s_first, s_all - s_first) * (1.0 / HEAD_DIM)
        out.append((tile * lax.rsqrt(ms + RMS_EPS) * g[:, c0:c0 + LANES]).astype(BF16))
    return jnp.concatenate(out, axis=1)


def _const_spec(shape):
    zeros = (0,) * len(shape)
    return pl.BlockSpec(shape, lambda *_: zeros, pipeline_mode=pl.Buffered(1))


def _input_kernel(x_ref, g0_ref, b0_ref, w_in_ref, w_pool_ref, pscale_ref,
                  q_ref, k_ref, v_ref, pooled_ref, ubuf_ref, *, attn_width, pool_group):
    t = pl.program_id(1)
    tm = x_ref.shape[1]
    halo = MAX_POOL_WINDOW

    @pl.when(t == 0)
    def _():
        ubuf_ref[0:halo, :] = jnp.zeros((halo, ubuf_ref.shape[1]), F32)

    @pl.when(t > 0)
    def _():
        ubuf_ref[0:halo, :] = ubuf_ref[tm:tm + halo, :]

    starts = list(range(0, tm, ROW_SUB))
    xn = [_layer_norm(x_ref[0, r0:r0 + ROW_SUB, :], g0_ref[...], b0_ref[...]) for r0 in starts]
    projs = [jnp.dot(a.astype(BF16), w_in_ref[...], preferred_element_type=F32) for a in xn]

    a = attn_width
    for r0, proj in zip(starts, projs):
        rows = slice(r0, r0 + ROW_SUB)
        q_ref[0, rows, :] = (proj[:, :a] * (1.0 / math.sqrt(HEAD_DIM))).astype(BF16)
        k_ref[0, rows, :] = proj[:, a:2 * a].astype(BF16)
        v_ref[0, rows, :] = proj[:, 2 * a:3 * a].astype(BF16)
        u = proj[:, 3 * a:]
        ubuf_ref[halo + r0:halo + r0 + ROW_SUB, :] = u

        pos = t * tm + r0 + lax.broadcasted_iota(jnp.int32, (ROW_SUB, 1), 0)
        for g, w in enumerate(POOL_WINDOWS):
            lo, hi = g * pool_group, (g + 1) * pool_group
            win = u[:, lo:hi]
            for i in range(1, w):
                win = win + ubuf_ref[halo + r0 - i:halo + r0 - i + ROW_SUB, lo:hi]
            count = jnp.minimum(pos + 1, w).astype(F32)
            d = win / count - u[:, lo:hi]
            y = jnp.dot(d.astype(BF16), w_pool_ref[g], preferred_element_type=F32)
            pooled_ref[0, rows, lo:hi] = (y * pscale_ref[:, lo:hi]).astype(BF16)


def _attn_kernel(q_ref, k_ref, v_ref, tri_ref, bd_ref, o_ref, osum_ref, carry_ref,
                 t_a, t_b, t_c, sp_buf, cs_buf, w_buf):
    s_len, lg = q_ref.shape[1], q_ref.shape[2]
    nh = lg // HEAD_DIM
    n_sub = s_len // SUB_Q
    n_steps = n_sub // SUBS_PER_STEP
    past = WIN - SUB_Q
    head_subs = past // SUB_Q + 1
    head_steps = -(-head_subs // SUBS_PER_STEP)
    first_sub = head_steps * SUBS_PER_STEP
    nt = (((1,), (1,)), ((), ()))
    tn = (((0,), (0,)), ((), ()))

    row_w = lax.broadcasted_iota(jnp.int32, (WIN, 1), 0)
    lane = lax.broadcasted_iota(jnp.int32, (1, lg), 1)
    lane_q = lane % SUB_Q
    lane_head = lane // HEAD_DIM
    diag_valid = lax.broadcasted_iota(jnp.int32, (SUB_Q, 1), 0) < lane_q
    head_masks = [lane_head == h for h in range(nh)]

    def sub_rows(sl):
        return pl.ds(pl.multiple_of(sl * SUB_Q, SUB_Q), SUB_Q)

    def load_qbd(sl):
        q_s = q_ref[0, sub_rows(sl), :]
        return jnp.concatenate([q_s] * nh, axis=0) * bd_ref[...]

    def softplus(t):
        return jnp.maximum(t, 0.0) + jnp.log(1.0 + jnp.exp2(jnp.abs(t) * (-LOG2_E)))

    def head_blocks(res):
        out = res[0:SUB_Q]
        for h in range(1, nh):
            out = jnp.where(head_masks[h], res[h * SUB_Q:(h + 1) * SUB_Q], out)
        return out

    def store_out(sl, o):
        o_ref[0, sub_rows(sl), :] = o.astype(BF16)

    def windows(subs):
        wss = [pl.multiple_of(jnp.maximum(hi - WIN, 0), SUB_Q) for hi, _, _, _ in subs]
        zs = [lax.dot_general(k_ref[0, pl.ds(ws, WIN), :], qbd_t, nt, preferred_element_type=F32)
              for ws, (_, _, qbd_t, _) in zip(wss, subs)]
        ts = [jnp.where((ws + row_w) < jnp.minimum(q0 + lane_q, hi), z, -MASKED)
              for ws, z, (hi, q0, _, _) in zip(wss, zs, subs)]
        csums = [jnp.dot(tri_ref[...], softplus(t).astype(BF16), preferred_element_type=F32) for t in ts]
        ws_bf = [jnp.exp(t - csum - carry).astype(BF16)
                 for t, csum, (_, _, _, carry) in zip(ts, csums, subs)]
        ress = [lax.dot_general(w, v_ref[0, pl.ds(ws, WIN), :], tn, preferred_element_type=F32)
                for w, ws in zip(ws_bf, wss)]
        return [(head_blocks(res), carry + csum[0:1, :], ws)
                for res, csum, ws, (_, _, _, carry) in zip(ress, csums, wss, subs)]

    zero_carry = jnp.zeros((1, lg), F32)
    head_sls = list(range(first_sub))
    outs = windows([((sl + 1) * SUB_Q, sl * SUB_Q, load_qbd(sl), zero_carry) for sl in head_sls])
    for sl, (o, _, _) in zip(head_sls, outs):
        store_out(sl, o)

    def step_subs(step):
        return [step * SUBS_PER_STEP + u for u in range(SUBS_PER_STEP)]

    def window_start(sl):
        return pl.multiple_of(sl * SUB_Q - past, SUB_Q)

    def stage_qk_dots(step):
        return [lax.dot_general(k_ref[0, pl.ds(window_start(sl), WIN), :], load_qbd(sl), nt,
                                preferred_element_type=F32) for sl in step_subs(step)]

    def stage_qk_store(zs):
        for u, z in enumerate(zs):
            t_a[u] = jnp.concatenate([z[:past], jnp.where(diag_valid, z[past:], -MASKED)], axis=0)

    def stage_softplus():
        for u in range(SUBS_PER_STEP):
            t = t_a[u]
            sp_buf[u] = softplus(t).astype(BF16)
            t_b[u] = t

    def stage_csum_dots():
        return [jnp.dot(tri_ref[...], sp_buf[u], preferred_element_type=F32) for u in range(SUBS_PER_STEP)]

    def stage_csum_store(csums):
        for u, cs in enumerate(csums):
            cs_buf[u] = cs
            t_c[u] = t_b[u]

    def stage_weights(step):
        for u, sl in enumerate(step_subs(step)):
            cs = cs_buf[u]
            w_buf[u] = jnp.exp(t_c[u] - cs).astype(BF16).T
            carry_ref[pl.ds(sl, 1), :] = cs[0:1, :]

    def stage_pv(step):
        for u, sl in enumerate(step_subs(step)):
            res = jnp.dot(w_buf[u], v_ref[0, pl.ds(window_start(sl), WIN), :],
                          preferred_element_type=F32)
            o = head_blocks(res)
            osum_ref[sub_rows(sl), :] = o
            store_out(sl, o)

    last_step = n_steps - 1

    def iteration(i, filling):
        live = (lambda step: step >= head_steps) if filling else (lambda step: True)
        on_grid = lambda step: min(step, last_step) if isinstance(step, int) else jnp.minimum(step, last_step)
        csums = stage_csum_dots() if live(i - 2) else None
        if live(i - 4):
            stage_pv(on_grid(i - 4))
        zs = stage_qk_dots(on_grid(i))
        if live(i - 3):
            stage_weights(on_grid(i - 3))
        if csums is not None:
            stage_csum_store(csums)
        if live(i - 1):
            stage_softplus()
        stage_qk_store(zs)

    fill_end = head_steps + PIPE_DEPTH - 1
    for i in range(head_steps, fill_end):
        iteration(i, filling=True)

    def trip(j, _):
        for r in range(PIPE_UNROLL):
            iteration(fill_end + j * PIPE_UNROLL + r, filling=False)
        return 0

    lax.fori_loop(0, -(-(n_steps + PIPE_DEPTH - 1 - fill_end) // PIPE_UNROLL), trip, 0)

    @pl.when(jnp.min(carry_ref[first_sub:, :]) < CARRY_DONE)
    def _():
        def fix(sl, _):
            q0 = sl * SUB_Q
            qbd_t = load_qbd(sl)
            carry = carry_ref[pl.ds(sl, 1), :]

            def cond(st):
                return (st[0] > 0) & (st[1] < CARRY_DONE)

            def body(st):
                (o_new, carry_new, ws_new), = windows([(st[0], q0, qbd_t, st[2])])
                return ws_new, jnp.min(carry_new), carry_new, st[3] + o_new

            st = lax.while_loop(cond, body, (q0 - past, jnp.min(carry), carry, osum_ref[sub_rows(sl), :]))
            store_out(sl, st[3])
            return 0

        lax.fori_loop(first_sub, n_sub, fix, 0)


def _output_kernel(x_ref, o_ref, pooled_ref, p_ref, ga_ref,
                   g0_ref, b0_ref, w_out_ref, g1_ref, b1_ref,
                   w_up_ref, w_down_ref, g2_ref, b2_ref,
                   w_ple_ref, w_gate_ref, g3_ref, b3_ref,
                   out_ref, *, alpha):
    tm = x_ref.shape[1]
    subs = [pl.ds(r0, ROW_SUB) for r0 in range(0, tm, ROW_SUB)]
    dot = functools.partial(jnp.dot, preferred_element_type=F32)

    xn = [_layer_norm(x_ref[0, rows, :], g0_ref[...], b0_ref[...]) for rows in subs]
    heads = [_head_rms_norm(o_ref[0, rows, :].astype(F32), ga_ref[...]) for rows in subs]
    mixed = [dot(jnp.concatenate([a, pooled_ref[0, rows, :]], axis=-1), w_out_ref[...])
             for a, rows in zip(heads, subs)]
    x1 = [_layer_norm(alpha * a + m, g1_ref[...], b1_ref[...]) for a, m in zip(xn, mixed)]

    x1b = [a.astype(BF16) for a in x1]
    d_ff = w_up_ref.shape[1]
    y = [None] * len(subs)
    for c0 in range(0, d_ff, FF_CHUNK):
        h = [dot(a, w_up_ref[:, c0:c0 + FF_CHUNK]) for a in x1b]
        h = [jnp.square(jnp.maximum(a, 0.0)).astype(BF16) for a in h]
        part = [dot(a, w_down_ref[c0:c0 + FF_CHUNK, :]) for a in h]
        y = [b if a is None else a + b for a, b in zip(y, part)]
    x2 = [_layer_norm(alpha * a + b, g2_ref[...], b2_ref[...]) for a, b in zip(x1, y)]

    gate = [jax.nn.sigmoid(dot(a.astype(BF16), w_gate_ref[...])) for a in x2]
    ple = [dot(p_ref[0, rows, :].astype(BF16), w_ple_ref[...]) * gt for rows, gt in zip(subs, gate)]
    for rows, a, b in zip(subs, x2, ple):
        out_ref[0, rows, :] = _layer_norm(alpha * a + b, g3_ref[...], b3_ref[...])


def _layer(x, p_i, g0, b0, w_in, attn_g, w_pool, pool_scale, w_out, ln1_g, ln1_b,
           w_up, w_down, ln2_g, ln2_b, w_ple, w_gate, ln3_g, ln3_b, alpha):
    B, S, D = x.shape
    n_groups, pool_group, _ = w_pool.shape
    pool_width = n_groups * pool_group
    attn_width = (w_in.shape[1] - pool_width) // 3
    mix_width = attn_width + pool_width
    d_ff = w_up.shape[1]
    ple_dim = p_i.shape[-1]
    assert S % TM_IN == 0 and S % TM_OUT == 0 and S % (SUB_Q * SUBS_PER_STEP) == 0
    assert attn_width % ATT_LANES == 0 and WIN % SUB_Q == 0 and S >= 2 * WIN
    assert d_ff % FF_CHUNK == 0 and ROW_SUB >= MAX_POOL_WINDOW and TM_IN % ROW_SUB == 0 and TM_OUT % ROW_SUB == 0

    row = lambda v: v.reshape(1, -1).astype(F32)
    cparams = functools.partial(pltpu.CompilerParams, vmem_limit_bytes=VMEM_LIMIT_BYTES)

    act = lambda width: jax.ShapeDtypeStruct((B, S, width), BF16)
    q, k, v, pooled = pl.pallas_call(
        functools.partial(_input_kernel, attn_width=attn_width, pool_group=pool_group),
        grid=(B, S // TM_IN),
        in_specs=[
            pl.BlockSpec((1, TM_IN, D), lambda b, t: (b, t, 0)),
            _const_spec((1, D)), _const_spec((1, D)),
            _const_spec(w_in.shape), _const_spec(w_pool.shape), _const_spec((1, pool_width)),
        ],
        out_specs=[
            pl.BlockSpec((1, TM_IN, attn_width), lambda b, t: (b, t, 0)),
            pl.BlockSpec((1, TM_IN, attn_width), lambda b, t: (b, t, 0)),
            pl.BlockSpec((1, TM_IN, attn_width), lambda b, t: (b, t, 0)),
            pl.BlockSpec((1, TM_IN, pool_width), lambda b, t: (b, t, 0)),
        ],
        out_shape=[act(attn_width), act(attn_width), act(attn_width), act(pool_width)],
        scratch_shapes=[pltpu.VMEM((TM_IN + MAX_POOL_WINDOW, pool_width), F32)],
        compiler_params=cparams(dimension_semantics=("parallel", "arbitrary")),
        name="hymba_input_stage",
    )(x, row(g0), row(b0), w_in.astype(BF16), w_pool.astype(BF16), row(pool_scale))

    n_groups_attn = attn_width // ATT_LANES
    suffix_ones = jnp.triu(jnp.ones((WIN, WIN), BF16))
    head_of = jnp.arange(ATT_LANES, dtype=jnp.int32)
    block_diag = (head_of[:, None] // SUB_Q == head_of[None, :] // HEAD_DIM).astype(BF16)
    seq_spec = pl.BlockSpec((1, S, ATT_LANES), lambda b, hg: (b, 0, hg))
    o = pl.pallas_call(
        _attn_kernel,
        grid=(B, n_groups_attn),
        in_specs=[seq_spec, seq_spec, seq_spec, _const_spec((WIN, WIN)), _const_spec((ATT_LANES, ATT_LANES))],
        out_specs=seq_spec,
        out_shape=act(attn_width),
        scratch_shapes=[pltpu.VMEM((S, ATT_LANES), F32), pltpu.VMEM((S // SUB_Q, ATT_LANES), F32),
                        pltpu.VMEM((SUBS_PER_STEP, WIN, ATT_LANES), F32),
                        pltpu.VMEM((SUBS_PER_STEP, WIN, ATT_LANES), F32),
                        pltpu.VMEM((SUBS_PER_STEP, WIN, ATT_LANES), F32),
                        pltpu.VMEM((SUBS_PER_STEP, WIN, ATT_LANES), BF16),
                        pltpu.VMEM((SUBS_PER_STEP, WIN, ATT_LANES), F32),
                        pltpu.VMEM((SUBS_PER_STEP, WIN, ATT_LANES), BF16)],
        compiler_params=cparams(dimension_semantics=("parallel", "parallel")),
        name="hymba_stickbreak_attention",
    )(q, k, v, suffix_ones, block_diag)

    out = pl.pallas_call(
        functools.partial(_output_kernel, alpha=alpha),
        grid=(B, S // TM_OUT),
        in_specs=[
            pl.BlockSpec((1, TM_OUT, D), lambda b, t: (b, t, 0)),
            pl.BlockSpec((1, TM_OUT, attn_width), lambda b, t: (b, t, 0)),
            pl.BlockSpec((1, TM_OUT, pool_width), lambda b, t: (b, t, 0)),
            pl.BlockSpec((1, TM_OUT, ple_dim), lambda b, t: (b, t, 0)),
            _const_spec((1, attn_width)),
            _const_spec((1, D)), _const_spec((1, D)),
            _const_spec((mix_width, D)), _const_spec((1, D)), _const_spec((1, D)),
            _const_spec((D, d_ff)), _const_spec((d_ff, D)), _const_spec((1, D)), _const_spec((1, D)),
            _const_spec((ple_dim, D)), _const_spec((D, D)), _const_spec((1, D)), _const_spec((1, D)),
        ],
        out_specs=pl.BlockSpec((1, TM_OUT, D), lambda b, t: (b, t, 0)),
        out_shape=jax.ShapeDtypeStruct((B, S, D), x.dtype),
        compiler_params=cparams(dimension_semantics=("parallel", "parallel")),
        name="hymba_output_stage",
    )(x, o, pooled, p_i, row(attn_g),
      row(g0), row(b0), w_out.astype(BF16), row(ln1_g), row(ln1_b),
      w_up.astype(BF16), w_down.astype(BF16), row(ln2_g), row(ln2_b),
      w_ple.astype(BF16), w_gate.astype(BF16), row(ln3_g), row(ln3_b))
    return out


def kernel(x, p, emb_ln_g, emb_ln_b, w_in, attn_out_g, w_pool, pool_scale, w_out, ln1_g, ln1_b, w_up, w_down, ln2_g, ln2_b, w_ple, w_ple_gate, ln3_g, ln3_b):
    depth = w_in.shape[0]
    assert depth == 1, "the fused output stage re-derives the layer input from x (single layer)"
    alpha = float((2 * depth) ** 0.25)
    return _layer(x, p[0], emb_ln_g, emb_ln_b, w_in[0], attn_out_g[0], w_pool[0], pool_scale[0],
                  w_out[0], ln1_g[0], ln1_b[0], w_up[0], w_down[0], ln2_g[0], ln2_b[0],
                  w_ple[0], w_ple_gate[0], ln3_g[0], ln3_b[0], alpha)
```

```python
import functools
import math

import jax
import jax.numpy as jnp
from jax import lax
from jax.experimental import pallas as pl
from jax.experimental.pallas import tpu as pltpu

HEAD_DIM = 64
POOL_WINDOWS = (2, 4, 8, 16)
LN_EPS = 1e-5
RMS_EPS = 1e-6
CARRY_DONE = 104.0
LOG2_E = 1.4426950408889634
MASKED = 1e30

LANES = 128
MXU_DIM = 256
MAX_POOL_WINDOW = max(POOL_WINDOWS)
VMEM_LIMIT_BYTES = 56 * 1024 * 1024

TM_IN = 1024
ATT_LANES = MXU_DIM
SUB_Q = ATT_LANES // (ATT_LANES // HEAD_DIM)
WIN = MXU_DIM
SUBS_PER_STEP = 2
PIPE_DEPTH = 5
PIPE_UNROLL = 5
TM_OUT = 512
ROW_SUB = MXU_DIM
FF_CHUNK = 1024

BF16 = jnp.bfloat16
F32 = jnp.float32


def _layer_norm(x, g, b):
    mu = jnp.mean(x, axis=-1, keepdims=True)
    xc = x - mu
    var = jnp.mean(xc * xc, axis=-1, keepdims=True)
    return xc * lax.rsqrt(var + LN_EPS) * g + b


def _head_rms_norm(o, g):
    lane = lax.broadcasted_iota(jnp.int32, (1, LANES), 1)
    first = lane < HEAD_DIM
    out = []
    for c0 in range(0, o.shape[1], LANES):
        tile = o[:, c0:c0 + LANES]
        sq = tile * tile
        s_all = jnp.sum(sq, axis=1, keepdims=True)
        s_first = jnp.sum(jnp.where(first, sq, 0.0), axis=1, keepdims=True)
        ms = jnp.where(first, s_first, s_all - s_first) * (1.0 / HEAD_DIM)
        out.append((tile * lax.rsqrt(ms + RMS_EPS) * g[:, c0:c0 + LANES]).astype(BF16))
    return jnp.concatenate(out, axis=1)


def _const_spec(shape):
    zeros = (0,) * len(shape)
    return pl.BlockSpec(shape, lambda *_: zeros, pipeline_mode=pl.Buffered(1))


def _input_kernel(x_ref, g0_ref, b0_ref, w_in_ref, w_pool_ref, pscale_ref,
                  q_ref, k_ref, v_ref, pooled_ref, ubuf_ref, *, attn_width, pool_group):
    t = pl.program_id(1)
    tm = x_ref.shape[1]
    halo = MAX_POOL_WINDOW

    @pl.when(t == 0)
    def _():
        ubuf_ref[0:halo, :] = jnp.zeros((halo, ubuf_ref.shape[1]), F32)

    @pl.when(t > 0)
    def _():
        ubuf_ref[0:halo, :] = ubuf_ref[tm:tm + halo, :]

    starts = list(range(0, tm, ROW_SUB))
    xn = [_layer_norm(x_ref[0, r0:r0 + ROW_SUB, :], g0_ref[...], b0_ref[...]) for r0 in starts]
    projs = [jnp.dot(a.astype(BF16), w_in_ref[...], preferred_element_type=F32) for a in xn]

    a = attn_width
    for r0, proj in zip(starts, projs):
        rows = slice(r0, r0 + ROW_SUB)
        q_ref[0, rows, :] = (proj[:, :a] * (1.0 / math.sqrt(HEAD_DIM))).astype(BF16)
        k_ref[0, rows, :] = proj[:, a:2 * a].astype(BF16)
        v_ref[0, rows, :] = proj[:, 2 * a:3 * a].astype(BF16)
        u = proj[:, 3 * a:]
        ubuf_ref[halo + r0:halo + r0 + ROW_SUB, :] = u

        pos = t * tm + r0 + lax.broadcasted_iota(jnp.int32, (ROW_SUB, 1), 0)
        for g, w in enumerate(POOL_WINDOWS):
            lo, hi = g * pool_group, (g + 1) * pool_group
            win = u[:, lo:hi]
            for i in range(1, w):
                win = win + ubuf_ref[halo + r0 - i:halo + r0 - i + ROW_SUB, lo:hi]
            count = jnp.minimum(pos + 1, w).astype(F32)
            d = win / count - u[:, lo:hi]
            y = jnp.dot(d.astype(BF16), w_pool_ref[g], preferred_element_type=F32)
            pooled_ref[0, rows, lo:hi] = (y * pscale_ref[:, lo:hi]).astype(BF16)


def _attn_kernel(q_ref, k_ref, v_ref, tri_ref, bd_ref, o_ref, osum_ref, carry_ref,
                 t_a, t_b, t_c, sp_buf, cs_buf, w_buf):
    s_len, lg = q_ref.shape[1], q_ref.shape[2]
    nh = lg // HEAD_DIM
    n_sub = s_len // SUB_Q
    n_steps = n_sub // SUBS_PER_STEP
    past = WIN - SUB_Q
    head_subs = past // SUB_Q + 1
    head_steps = -(-head_subs // SUBS_PER_STEP)
    first_sub = head_steps * SUBS_PER_STEP
    nt = (((1,), (1,)), ((), ()))
    tn = (((0,), (0,)), ((), ()))

    row_w = lax.broadcasted_iota(jnp.int32, (WIN, 1), 0)
    lane = lax.broadcasted_iota(jnp.int32, (1, lg), 1)
    lane_q = lane % SUB_Q
    lane_head = lane // HEAD_DIM
    diag_valid = lax.broadcasted_iota(jnp.int32, (SUB_Q, 1), 0) < lane_q
    head_masks = [lane_head == h for h in range(nh)]

    def sub_rows(sl):
        return pl.ds(pl.multiple_of(sl * SUB_Q, SUB_Q), SUB_Q)

    def load_qbd(sl):
        q_s = q_ref[0, sub_rows(sl), :]
        return jnp.concatenate([q_s] * nh, axis=0) * bd_ref[...]

    def softplus(t):
        return jnp.maximum(t, 0.0) + jnp.log(1.0 + jnp.exp2(jnp.abs(t) * (-LOG2_E)))

    def head_blocks(res):
        out = res[0:SUB_Q]
        for h in range(1, nh):
            out = jnp.where(head_masks[h], res[h * SUB_Q:(h + 1) * SUB_Q], out)
        return out

    def store_out(sl, o):
        o_ref[0, sub_rows(sl), :] = o.astype(BF16)

    def windows(subs):
        wss = [pl.multiple_of(jnp.maximum(hi - WIN, 0), SUB_Q) for hi, _, _, _ in subs]
        zs = [lax.dot_general(k_ref[0, pl.ds(ws, WIN), :], qbd_t, nt, preferred_element_type=F32)
              for ws, (_, _, qbd_t, _) in zip(wss, subs)]
        ts = [jnp.where((ws + row_w) < jnp.minimum(q0 + lane_q, hi), z, -MASKED)
              for ws, z, (hi, q0, _, _) in zip(wss, zs, subs)]
        csums = [jnp.dot(tri_ref[...], softplus(t).astype(BF16), preferred_element_type=F32) for t in ts]
        ws_bf = [jnp.exp(t - csum - carry).astype(BF16)
                 for t, csum, (_, _, _, carry) in zip(ts, csums, subs)]
        ress = [lax.dot_general(w, v_ref[0, pl.ds(ws, WIN), :], tn, preferred_element_type=F32)
                for w, ws in zip(ws_bf, wss)]
        return [(head_blocks(res), carry + csum[0:1, :], ws)
                for res, csum, ws, (_, _, _, carry) in zip(ress, csums, wss, subs)]

    zero_carry = jnp.zeros((1, lg), F32)
    head_sls = list(range(first_sub))
    outs = windows([((sl + 1) * SUB_Q, sl * SUB_Q, load_qbd(sl), zero_carry) for sl in head_sls])
    for sl, (o, _, _) in zip(head_sls, outs):
        store_out(sl, o)

    def step_subs(step):
        return [step * SUBS_PER_STEP + u for u in range(SUBS_PER_STEP)]

    def window_start(sl):
        return pl.multiple_of(sl * SUB_Q - past, SUB_Q)

    def stage_qk_dots(step):
        return [lax.dot_general(k_ref[0, pl.ds(window_start(sl), WIN), :], load_qbd(sl), nt,
                                preferred_element_type=F32) for sl in step_subs(step)]

    def stage_qk_store(zs):
        for u, z in enumerate(zs):
            t_a[u] = jnp.concatenate([z[:past], jnp.where(diag_valid, z[past:], -MASKED)], axis=0)

    def stage_softplus():
        for u in range(SUBS_PER_STEP):
            t = t_a[u]
            sp_buf[u] = softplus(t).astype(BF16)
            t_b[u] = t

    def stage_csum_dots():
        return [jnp.dot(tri_ref[...], sp_buf[u], preferred_element_type=F32) for u in range(SUBS_PER_STEP)]

    def stage_csum_store(csums):
        for u, cs in enumerate(csums):
            cs_buf[u] = cs
            t_c[u] = t_b[u]

    def stage_weights(step):
        for u, sl in enumerate(step_subs(step)):
            cs = cs_buf[u]
            w_buf[u] = jnp.exp(t_c[u] - cs).astype(BF16).T
            carry_ref[pl.ds(sl, 1), :] = cs[0:1, :]

    def stage_pv(step):
        for u, sl in enumerate(step_subs(step)):
            res = jnp.dot(w_buf[u], v_ref[0, pl.ds(window_start(sl), WIN), :],
                          preferred_element_type=F32)
            o = head_blocks(res)
            osum_ref[sub_rows(sl), :] = o
            store_out(sl, o)

    def iteration(i, live):
        csums = stage_csum_dots() if live(i - 2) else None
        if live(i - 4):
            stage_pv(i - 4)
        zs = stage_qk_dots(i) if live(i) else None
        if live(i - 3):
            stage_weights(i - 3)
        if csums is not None:
            stage_csum_store(csums)
        if live(i - 1):
            stage_softplus()
        if zs is not None:
            stage_qk_store(zs)

    on_grid = lambda step: head_steps <= step < n_steps
    fill_end = head_steps + PIPE_DEPTH - 1
    trips = (n_steps - fill_end) // PIPE_UNROLL
    drain_start = fill_end + trips * PIPE_UNROLL
    for i in range(head_steps, fill_end):
        iteration(i, on_grid)

    def trip(j, _):
        for r in range(PIPE_UNROLL):
            iteration(fill_end + j * PIPE_UNROLL + r, lambda step: True)
        return 0

    lax.fori_loop(0, trips, trip, 0)
    for i in range(drain_start, n_steps + PIPE_DEPTH - 1):
        iteration(i, on_grid)

    @pl.when(jnp.min(carry_ref[first_sub:, :]) < CARRY_DONE)
    def _():
        def fix(sl, _):
            q0 = sl * SUB_Q
            qbd_t = load_qbd(sl)
            carry = carry_ref[pl.ds(sl, 1), :]

            def cond(st):
                return (st[0] > 0) & (st[1] < CARRY_DONE)

            def body(st):
                (o_new, carry_new, ws_new), = windows([(st[0], q0, qbd_t, st[2])])
                return ws_new, jnp.min(carry_new), carry_new, st[3] + o_new

            st = lax.while_loop(cond, body, (q0 - past, jnp.min(carry), carry, osum_ref[sub_rows(sl), :]))
            store_out(sl, st[3])
            return 0

        lax.fori_loop(first_sub, n_sub, fix, 0)


def _output_kernel(x_ref, o_ref, pooled_ref, p_ref, ga_ref,
                   g0_ref, b0_ref, w_out_ref, g1_ref, b1_ref,
                   w_up_ref, w_down_ref, g2_ref, b2_ref,
                   w_ple_ref, w_gate_ref, g3_ref, b3_ref,
                   out_ref, *, alpha):
    tm = x_ref.shape[1]
    subs = [pl.ds(r0, ROW_SUB) for r0 in range(0, tm, ROW_SUB)]
    dot = functools.partial(jnp.dot, preferred_element_type=F32)

    xn = [_layer_norm(x_ref[0, rows, :], g0_ref[...], b0_ref[...]) for rows in subs]
    heads = [_head_rms_norm(o_ref[0, rows, :].astype(F32), ga_ref[...]) for rows in subs]
    mixed = [dot(jnp.concatenate([a, pooled_ref[0, rows, :]], axis=-1), w_out_ref[...])
             for a, rows in zip(heads, subs)]
    x1 = [_layer_norm(alpha * a + m, g1_ref[...], b1_ref[...]) for a, m in zip(xn, mixed)]

    x1b = [a.astype(BF16) for a in x1]
    d_ff = w_up_ref.shape[1]
    y = [None] * len(subs)
    for c0 in range(0, d_ff, FF_CHUNK):
        h = [dot(a, w_up_ref[:, c0:c0 + FF_CHUNK]) for a in x1b]
        h = [jnp.square(jnp.maximum(a, 0.0)).astype(BF16) for a in h]
        part = [dot(a, w_down_ref[c0:c0 + FF_CHUNK, :]) for a in h]
        y = [b if a is None else a + b for a, b in zip(y, part)]
    x2 = [_layer_norm(alpha * a + b, g2_ref[...], b2_ref[...]) for a, b in zip(x1, y)]

    gate = [jax.nn.sigmoid(dot(a.astype(BF16), w_gate_ref[...])) for a in x2]
    ple = [dot(p_ref[0, rows, :].astype(BF16), w_ple_ref[...]) * gt for rows, gt in zip(subs, gate)]
    for rows, a, b in zip(subs, x2, ple):
        out_ref[0, rows, :] = _layer_norm(alpha * a + b, g3_ref[...], b3_ref[...])


def _layer(x, p_i, g0, b0, w_in, attn_g, w_pool, pool_scale, w_out, ln1_g, ln1_b,
           w_up, w_down, ln2_g, ln2_b, w_ple, w_gate, ln3_g, ln3_b, alpha):
    B, S, D = x.shape
    n_groups, pool_group, _ = w_pool.shape
    pool_width = n_groups * pool_group
    attn_width = (w_in.shape[1] - pool_width) // 3
    mix_width = attn_width + pool_width
    d_ff = w_up.shape[1]
    ple_dim = p_i.shape[-1]
    assert S % TM_IN == 0 and S % TM_OUT == 0 and S % (SUB_Q * SUBS_PER_STEP) == 0
    assert attn_width % ATT_LANES == 0 and WIN % SUB_Q == 0 and S >= 2 * WIN
    assert d_ff % FF_CHUNK == 0 and ROW_SUB >= MAX_POOL_WINDOW and TM_IN % ROW_SUB == 0 and TM_OUT % ROW_SUB == 0

    row = lambda v: v.reshape(1, -1).astype(F32)
    cparams = functools.partial(pltpu.CompilerParams, vmem_limit_bytes=VMEM_LIMIT_BYTES)

    act = lambda width: jax.ShapeDtypeStruct((B, S, width), BF16)
    q, k, v, pooled = pl.pallas_call(
        functools.partial(_input_kernel, attn_width=attn_width, pool_group=pool_group),
        grid=(B, S // TM_IN),
        in_specs=[
            pl.BlockSpec((1, TM_IN, D), lambda b, t: (b, t, 0)),
            _const_spec((1, D)), _const_spec((1, D)),
            _const_spec(w_in.shape), _const_spec(w_pool.shape), _const_spec((1, pool_width)),
        ],
        out_specs=[
            pl.BlockSpec((1, TM_IN, attn_width), lambda b, t: (b, t, 0)),
            pl.BlockSpec((1, TM_IN, attn_width), lambda b, t: (b, t, 0)),
            pl.BlockSpec((1, TM_IN, attn_width), lambda b, t: (b, t, 0)),
            pl.BlockSpec((1, TM_IN, pool_width), lambda b, t: (b, t, 0)),
        ],
        out_shape=[act(attn_width), act(attn_width), act(attn_width), act(pool_width)],
        scratch_shapes=[pltpu.VMEM((TM_IN + MAX_POOL_WINDOW, pool_width), F32)],
        compiler_params=cparams(dimension_semantics=("parallel", "arbitrary")),
        name="hymba_input_stage",
    )(x, row(g0), row(b0), w_in.astype(BF16), w_pool.astype(BF16), row(pool_scale))

    n_groups_attn = attn_width // ATT_LANES
    suffix_ones = jnp.triu(jnp.ones((WIN, WIN), BF16))
    head_of = jnp.arange(ATT_LANES, dtype=jnp.int32)
    block_diag = (head_of[:, None] // SUB_Q == head_of[None, :] // HEAD_DIM).astype(BF16)
    seq_spec = pl.BlockSpec((1, S, ATT_LANES), lambda b, hg: (b, 0, hg))
    o = pl.pallas_call(
        _attn_kernel,
        grid=(B, n_groups_attn),
        in_specs=[seq_spec, seq_spec, seq_spec, _const_spec((WIN, WIN)), _const_spec((ATT_LANES, ATT_LANES))],
        out_specs=seq_spec,
        out_shape=act(attn_width),
        scratch_shapes=[pltpu.VMEM((S, ATT_LANES), F32), pltpu.VMEM((S // SUB_Q, ATT_LANES), F32),
                        pltpu.VMEM((SUBS_PER_STEP, WIN, ATT_LANES), F32),
                        pltpu.VMEM((SUBS_PER_STEP, WIN, ATT_LANES), F32),
                        pltpu.VMEM((SUBS_PER_STEP, WIN, ATT_LANES), F32),
                        pltpu.VMEM((SUBS_PER_STEP, WIN, ATT_LANES), BF16),
                        pltpu.VMEM((SUBS_PER_STEP, WIN, ATT_LANES), F32),
                        pltpu.VMEM((SUBS_PER_STEP, WIN, ATT_LANES), BF16)],
        compiler_params=cparams(dimension_semantics=("parallel", "parallel")),
        name="hymba_stickbreak_attention",
    )(q, k, v, suffix_ones, block_diag)

    out = pl.pallas_call(
        functools.partial(_output_kernel, alpha=alpha),
        grid=(B, S // TM_OUT),
        in_specs=[
            pl.BlockSpec((1, TM_OUT, D), lambda b, t: (b, t, 0)),
            pl.BlockSpec((1, TM_OUT, attn_width), lambda b, t: (b, t, 0)),
            pl.BlockSpec((1, TM_OUT, pool_width), lambda b, t: (b, t, 0)),
            pl.BlockSpec((1, TM_OUT, ple_dim), lambda b, t: (b, t, 0)),
            _const_spec((1, attn_width)),
            _const_spec((1, D)), _const_spec((1, D)),
            _const_spec((mix_width, D)), _const_spec((1, D)), _const_spec((1, D)),
            _const_spec((D, d_ff)), _const_spec((d_ff, D)), _const_spec((1, D)), _const_spec((1, D)),
            _const_spec((ple_dim, D)), _const_spec((D, D)), _const_spec((1, D)), _const_spec((1, D)),
        ],
        out_specs=pl.BlockSpec((1, TM_OUT, D), lambda b, t: (b, t, 0)),
        out_shape=jax.ShapeDtypeStruct((B, S, D), x.dtype),
        compiler_params=cparams(dimension_semantics=("parallel", "parallel")),
        name="hymba_output_stage",
    )(x, o, pooled, p_i, row(attn_g),
      row(g0), row(b0), w_out.astype(BF16), row(ln1_g), row(ln1_b),
      w_up.astype(BF16), w_down.astype(BF16), row(ln2_g), row(ln2_b),
      w_ple.astype(BF16), w_gate.astype(BF16), row(ln3_g), row(ln3_b))
    return out


def kernel(x, p, emb_ln_g, emb_ln_b, w_in, attn_out_g, w_pool, pool_scale, w_out, ln1_g, ln1_b, w_up, w_down, ln2_g, ln2_b, w_ple, w_ple_gate, ln3_g, ln3_b):
    depth = w_in.shape[0]
    assert depth == 1, "the fused output stage re-derives the layer input from x (single layer)"
    alpha = float((2 * depth) ** 0.25)
    return _layer(x, p[0], emb_ln_g, emb_ln_b, w_in[0], attn_out_g[0], w_pool[0], pool_scale[0],
                  w_out[0], ln1_g[0], ln1_b[0], w_up[0], w_down[0], ln2_g[0], ln2_b[0],
                  w_ple[0], w_ple_gate[0], ln3_g[0], ln3_b[0], alpha)
```

```python
import functools
import math

import jax
import jax.numpy as jnp
from jax import lax
from jax.experimental import pallas as pl
from jax.experimental.pallas import tpu as pltpu

HEAD_DIM = 64
POOL_WINDOWS = (2, 4, 8, 16)
LN_EPS = 1e-5
RMS_EPS = 1e-6
CARRY_DONE = 104.0
LOG2_E = 1.4426950408889634
MASKED = 1e30

LANES = 128
MXU_DIM = 256
MAX_POOL_WINDOW = max(POOL_WINDOWS)
VMEM_LIMIT_BYTES = 56 * 1024 * 1024

TM_IN = 1024
ATT_LANES = MXU_DIM
SUB_Q = ATT_LANES // (ATT_LANES // HEAD_DIM)
WIN = MXU_DIM
SUBS_PER_STEP = 2
PIPE_DEPTH = 5
PIPE_UNROLL = 5
TM_OUT = 512
IN_SUB = MXU_DIM // 2
ROW_SUB = MXU_DIM
FF_CHUNK = 1024

BF16 = jnp.bfloat16
F32 = jnp.float32


def _layer_norm(x, g, b):
    mu = jnp.mean(x, axis=-1, keepdims=True)
    xc = x - mu
    var = jnp.mean(xc * xc, axis=-1, keepdims=True)
    return xc * lax.rsqrt(var + LN_EPS) * g + b


def _head_rms_norm(o, g):
    lane = lax.broadcasted_iota(jnp.int32, (1, LANES), 1)
    first = lane < HEAD_DIM
    out = []
    for c0 in range(0, o.shape[1], LANES):
        tile = o[:, c0:c0 + LANES]
        sq = tile * tile
        s_all = jnp.sum(sq, axis=1, keepdims=True)
        s_first = jnp.sum(jnp.where(first, sq, 0.0), axis=1, keepdims=True)
        ms = jnp.where(first, s_first, s_all - s_first) * (1.0 / HEAD_DIM)
        out.append((tile * lax.rsqrt(ms + RMS_EPS) * g[:, c0:c0 + LANES]).astype(BF16))
    return jnp.concatenate(out, axis=1)


def _const_spec(shape):
    zeros = (0,) * len(shape)
    return pl.BlockSpec(shape, lambda *_: zeros, pipeline_mode=pl.Buffered(1))


def _input_kernel(x_ref, g0_ref, b0_ref, w_in_ref, w_pool_ref, pscale_ref,
                  q_ref, k_ref, v_ref, pooled_ref, ubuf_ref, *, attn_width, pool_group):
    t = pl.program_id(1)
    tm = x_ref.shape[1]
    halo = MAX_POOL_WINDOW

    @pl.when(t == 0)
    def _():
        ubuf_ref[0:halo, :] = jnp.zeros((halo, ubuf_ref.shape[1]), F32)

    @pl.when(t > 0)
    def _():
        ubuf_ref[0:halo, :] = ubuf_ref[tm:tm + halo, :]

    starts = list(range(0, tm, IN_SUB))
    xn = [_layer_norm(x_ref[0, r0:r0 + IN_SUB, :], g0_ref[...], b0_ref[...]) for r0 in starts]
    projs = [jnp.dot(a.astype(BF16), w_in_ref[...], preferred_element_type=F32) for a in xn]

    a = attn_width
    for r0, proj in zip(starts, projs):
        rows = slice(r0, r0 + IN_SUB)
        q_ref[0, rows, :] = (proj[:, :a] * (1.0 / math.sqrt(HEAD_DIM))).astype(BF16)
        k_ref[0, rows, :] = proj[:, a:2 * a].astype(BF16)
        v_ref[0, rows, :] = proj[:, 2 * a:3 * a].astype(BF16)
        u = proj[:, 3 * a:]
        ubuf_ref[halo + r0:halo + r0 + IN_SUB, :] = u

        pos = t * tm + r0 + lax.broadcasted_iota(jnp.int32, (IN_SUB, 1), 0)
        for g, w in enumerate(POOL_WINDOWS):
            lo, hi = g * pool_group, (g + 1) * pool_group
            win = u[:, lo:hi]
            for i in range(1, w):
                win = win + ubuf_ref[halo + r0 - i:halo + r0 - i + IN_SUB, lo:hi]
            count = jnp.minimum(pos + 1, w).astype(F32)
            d = win / count - u[:, lo:hi]
            y = jnp.dot(d.astype(BF16), w_pool_ref[g], preferred_element_type=F32)
            pooled_ref[0, rows, lo:hi] = (y * pscale_ref[:, lo:hi]).astype(BF16)


def _attn_kernel(q_ref, k_ref, v_ref, tri_ref, bd_ref, o_ref, osum_ref, carry_ref,
                 t_a, t_b, t_c, sp_buf, cs_buf, w_buf):
    s_len, lg = q_ref.shape[1], q_ref.shape[2]
    nh = lg // HEAD_DIM
    n_sub = s_len // SUB_Q
    n_steps = n_sub // SUBS_PER_STEP
    past = WIN - SUB_Q
    head_subs = past // SUB_Q + 1
    head_steps = -(-head_subs // SUBS_PER_STEP)
    first_sub = head_steps * SUBS_PER_STEP
    nt = (((1,), (1,)), ((), ()))
    tn = (((0,), (0,)), ((), ()))

    row_w = lax.broadcasted_iota(jnp.int32, (WIN, 1), 0)
    lane = lax.broadcasted_iota(jnp.int32, (1, lg), 1)
    lane_q = lane % SUB_Q
    lane_head = lane // HEAD_DIM
    diag_valid = lax.broadcasted_iota(jnp.int32, (SUB_Q, 1), 0) < lane_q
    head_masks = [lane_head == h for h in range(nh)]

    def sub_rows(sl):
        return pl.ds(pl.multiple_of(sl * SUB_Q, SUB_Q), SUB_Q)

    def load_qbd(sl):
        q_s = q_ref[0, sub_rows(sl), :]
        return jnp.concatenate([q_s] * nh, axis=0) * bd_ref[...]

    def softplus(t):
        return jnp.maximum(t, 0.0) + jnp.log(1.0 + jnp.exp2(jnp.abs(t) * (-LOG2_E)))

    def head_blocks(res):
        out = res[0:SUB_Q]
        for h in range(1, nh):
            out = jnp.where(head_masks[h], res[h * SUB_Q:(h + 1) * SUB_Q], out)
        return out

    def store_out(sl, o):
        o_ref[0, sub_rows(sl), :] = o.astype(BF16)

    def windows(subs):
        wss = [pl.multiple_of(jnp.maximum(hi - WIN, 0), SUB_Q) for hi, _, _, _ in subs]
        zs = [lax.dot_general(k_ref[0, pl.ds(ws, WIN), :], qbd_t, nt, preferred_element_type=F32)
              for ws, (_, _, qbd_t, _) in zip(wss, subs)]
        ts = [jnp.where((ws + row_w) < jnp.minimum(q0 + lane_q, hi), z, -MASKED)
              for ws, z, (hi, q0, _, _) in zip(wss, zs, subs)]
        csums = [jnp.dot(tri_ref[...], softplus(t).astype(BF16), preferred_element_type=F32) for t in ts]
        ws_bf = [jnp.exp(t - csum - carry).astype(BF16)
                 for t, csum, (_, _, _, carry) in zip(ts, csums, subs)]
        ress = [lax.dot_general(w, v_ref[0, pl.ds(ws, WIN), :], tn, preferred_element_type=F32)
                for w, ws in zip(ws_bf, wss)]
        return [(head_blocks(res), carry + csum[0:1, :], ws)
                for res, csum, ws, (_, _, _, carry) in zip(ress, csums, wss, subs)]

    zero_carry = jnp.zeros((1, lg), F32)
    head_sls = list(range(first_sub))
    outs = windows([((sl + 1) * SUB_Q, sl * SUB_Q, load_qbd(sl), zero_carry) for sl in head_sls])
    for sl, (o, _, _) in zip(head_sls, outs):
        store_out(sl, o)

    def step_subs(step):
        return [step * SUBS_PER_STEP + u for u in range(SUBS_PER_STEP)]

    def window_start(sl):
        return pl.multiple_of(sl * SUB_Q - past, SUB_Q)

    def stage_qk_dots(step):
        return [lax.dot_general(k_ref[0, pl.ds(window_start(sl), WIN), :], load_qbd(sl), nt,
                                preferred_element_type=F32) for sl in step_subs(step)]

    def stage_qk_store(zs):
        for u, z in enumerate(zs):
            t_a[u] = jnp.concatenate([z[:past], jnp.where(diag_valid, z[past:], -MASKED)], axis=0)

    def stage_softplus():
        for u in range(SUBS_PER_STEP):
            t = t_a[u]
            sp_buf[u] = softplus(t).astype(BF16)
            t_b[u] = t

    def stage_csum_dots():
        return [jnp.dot(tri_ref[...], sp_buf[u], preferred_element_type=F32) for u in range(SUBS_PER_STEP)]

    def stage_csum_store(csums):
        for u, cs in enumerate(csums):
            cs_buf[u] = cs
            t_c[u] = t_b[u]

    def stage_weights(step):
        for u, sl in enumerate(step_subs(step)):
            cs = cs_buf[u]
            w_buf[u] = jnp.exp(t_c[u] - cs).astype(BF16).T
            carry_ref[pl.ds(sl, 1), :] = cs[0:1, :]

    def stage_pv(step):
        for u, sl in enumerate(step_subs(step)):
            res = jnp.dot(w_buf[u], v_ref[0, pl.ds(window_start(sl), WIN), :],
                          preferred_element_type=F32)
            o = head_blocks(res)
            osum_ref[sub_rows(sl), :] = o
            store_out(sl, o)

    def iteration(i, live):
        csums = stage_csum_dots() if live(i - 2) else None
        if live(i - 4):
            stage_pv(i - 4)
        zs = stage_qk_dots(i) if live(i) else None
        if live(i - 3):
            stage_weights(i - 3)
        if csums is not None:
            stage_csum_store(csums)
        if live(i - 1):
            stage_softplus()
        if zs is not None:
            stage_qk_store(zs)

    on_grid = lambda step: head_steps <= step < n_steps
    fill_end = head_steps + PIPE_DEPTH - 1
    trips = (n_steps - fill_end) // PIPE_UNROLL
    drain_start = fill_end + trips * PIPE_UNROLL
    for i in range(head_steps, fill_end):
        iteration(i, on_grid)

    def trip(j, _):
        for r in range(PIPE_UNROLL):
            iteration(fill_end + j * PIPE_UNROLL + r, lambda step: True)
        return 0

    lax.fori_loop(0, trips, trip, 0)
    for i in range(drain_start, n_steps + PIPE_DEPTH - 1):
        iteration(i, on_grid)

    @pl.when(jnp.min(carry_ref[first_sub:, :]) < CARRY_DONE)
    def _():
        def fix(sl, _):
            q0 = sl * SUB_Q
            qbd_t = load_qbd(sl)
            carry = carry_ref[pl.ds(sl, 1), :]

            def cond(st):
                return (st[0] > 0) & (st[1] < CARRY_DONE)

            def body(st):
                (o_new, carry_new, ws_new), = windows([(st[0], q0, qbd_t, st[2])])
                return ws_new, jnp.min(carry_new), carry_new, st[3] + o_new

            st = lax.while_loop(cond, body, (q0 - past, jnp.min(carry), carry, osum_ref[sub_rows(sl), :]))
            store_out(sl, st[3])
            return 0

        lax.fori_loop(first_sub, n_sub, fix, 0)


def _output_kernel(x_ref, o_ref, pooled_ref, p_ref, ga_ref,
                   g0_ref, b0_ref, w_out_ref, g1_ref, b1_ref,
                   w_up_ref, w_down_ref, g2_ref, b2_ref,
                   w_ple_ref, w_gate_ref, g3_ref, b3_ref,
                   out_ref, *, alpha):
    tm = x_ref.shape[1]
    subs = [pl.ds(r0, ROW_SUB) for r0 in range(0, tm, ROW_SUB)]
    dot = functools.partial(jnp.dot, preferred_element_type=F32)

    xn = [_layer_norm(x_ref[0, rows, :], g0_ref[...], b0_ref[...]) for rows in subs]
    heads = [_head_rms_norm(o_ref[0, rows, :].astype(F32), ga_ref[...]) for rows in subs]
    mixed = [dot(jnp.concatenate([a, pooled_ref[0, rows, :]], axis=-1), w_out_ref[...])
             for a, rows in zip(heads, subs)]
    x1 = [_layer_norm(alpha * a + m, g1_ref[...], b1_ref[...]) for a, m in zip(xn, mixed)]

    x1b = [a.astype(BF16) for a in x1]
    d_ff = w_up_ref.shape[1]
    y = [None] * len(subs)
    for c0 in range(0, d_ff, FF_CHUNK):
        h = [dot(a, w_up_ref[:, c0:c0 + FF_CHUNK]) for a in x1b]
        h = [jnp.square(jnp.maximum(a, 0.0)).astype(BF16) for a in h]
        part = [dot(a, w_down_ref[c0:c0 + FF_CHUNK, :]) for a in h]
        y = [b if a is None else a + b for a, b in zip(y, part)]
    x2 = [_layer_norm(alpha * a + b, g2_ref[...], b2_ref[...]) for a, b in zip(x1, y)]

    gate = [jax.nn.sigmoid(dot(a.astype(BF16), w_gate_ref[...])) for a in x2]
    ple = [dot(p_ref[0, rows, :].astype(BF16), w_ple_ref[...]) * gt for rows, gt in zip(subs, gate)]
    for rows, a, b in zip(subs, x2, ple):
        out_ref[0, rows, :] = _layer_norm(alpha * a + b, g3_ref[...], b3_ref[...])


def _layer(x, p_i, g0, b0, w_in, attn_g, w_pool, pool_scale, w_out, ln1_g, ln1_b,
           w_up, w_down, ln2_g, ln2_b, w_ple, w_gate, ln3_g, ln3_b, alpha):
    B, S, D = x.shape
    n_groups, pool_group, _ = w_pool.shape
    pool_width = n_groups * pool_group
    attn_width = (w_in.shape[1] - pool_width) // 3
    mix_width = attn_width + pool_width
    d_ff = w_up.shape[1]
    ple_dim = p_i.shape[-1]
    assert S % TM_IN == 0 and S % TM_OUT == 0 and S % (SUB_Q * SUBS_PER_STEP) == 0
    assert attn_width % ATT_LANES == 0 and WIN % SUB_Q == 0 and S >= 2 * WIN
    assert d_ff % FF_CHUNK == 0 and IN_SUB >= MAX_POOL_WINDOW and TM_IN % IN_SUB == 0 and TM_OUT % ROW_SUB == 0

    row = lambda v: v.reshape(1, -1).astype(F32)
    cparams = functools.partial(pltpu.CompilerParams, vmem_limit_bytes=VMEM_LIMIT_BYTES)

    act = lambda width: jax.ShapeDtypeStruct((B, S, width), BF16)
    q, k, v, pooled = pl.pallas_call(
        functools.partial(_input_kernel, attn_width=attn_width, pool_group=pool_group),
        grid=(B, S // TM_IN),
        in_specs=[
            pl.BlockSpec((1, TM_IN, D), lambda b, t: (b, t, 0)),
            _const_spec((1, D)), _const_spec((1, D)),
            _const_spec(w_in.shape), _const_spec(w_pool.shape), _const_spec((1, pool_width)),
        ],
        out_specs=[
            pl.BlockSpec((1, TM_IN, attn_width), lambda b, t: (b, t, 0)),
            pl.BlockSpec((1, TM_IN, attn_width), lambda b, t: (b, t, 0)),
            pl.BlockSpec((1, TM_IN, attn_width), lambda b, t: (b, t, 0)),
            pl.BlockSpec((1, TM_IN, pool_width), lambda b, t: (b, t, 0)),
        ],
        out_shape=[act(attn_width), act(attn_width), act(attn_width), act(pool_width)],
        scratch_shapes=[pltpu.VMEM((TM_IN + MAX_POOL_WINDOW, pool_width), F32)],
        compiler_params=cparams(dimension_semantics=("parallel", "arbitrary")),
        name="hymba_input_stage",
    )(x, row(g0), row(b0), w_in.astype(BF16), w_pool.astype(BF16), row(pool_scale))

    n_groups_attn = attn_width // ATT_LANES
    suffix_ones = jnp.triu(jnp.ones((WIN, WIN), BF16))
    head_of = jnp.arange(ATT_LANES, dtype=jnp.int32)
    block_diag = (head_of[:, None] // SUB_Q == head_of[None, :] // HEAD_DIM).astype(BF16)
    seq_spec = pl.BlockSpec((1, S, ATT_LANES), lambda b, hg: (b, 0, hg))
    o = pl.pallas_call(
        _attn_kernel,
        grid=(B, n_groups_attn),
        in_specs=[seq_spec, seq_spec, seq_spec, _const_spec((WIN, WIN)), _const_spec((ATT_LANES, ATT_LANES))],
        out_specs=seq_spec,
        out_shape=act(attn_width),
        scratch_shapes=[pltpu.VMEM((S, ATT_LANES), F32), pltpu.VMEM((S // SUB_Q, ATT_LANES), F32),
                        pltpu.VMEM((SUBS_PER_STEP, WIN, ATT_LANES), F32),
                        pltpu.VMEM((SUBS_PER_STEP, WIN, ATT_LANES), F32),
                        pltpu.VMEM((SUBS_PER_STEP, WIN, ATT_LANES), F32),
                        pltpu.VMEM((SUBS_PER_STEP, WIN, ATT_LANES), BF16),
                        pltpu.VMEM((SUBS_PER_STEP, WIN, ATT_LANES), F32),
                        pltpu.VMEM((SUBS_PER_STEP, WIN, ATT_LANES), BF16)],
        compiler_params=cparams(dimension_semantics=("parallel", "parallel")),
        name="hymba_stickbreak_attention",
    )(q, k, v, suffix_ones, block_diag)

    out = pl.pallas_call(
        functools.partial(_output_kernel, alpha=alpha),
        grid=(B, S // TM_OUT),
        in_specs=[
            pl.BlockSpec((1, TM_OUT, D), lambda b, t: (b, t, 0)),
            pl.BlockSpec((1, TM_OUT, attn_width), lambda b, t: (b, t, 0)),
            pl.BlockSpec((1, TM_OUT, pool_width), lambda b, t: (b, t, 0)),
            pl.BlockSpec((1, TM_OUT, ple_dim), lambda b, t: (b, t, 0)),
            _const_spec((1, attn_width)),
            _const_spec((1, D)), _const_spec((1, D)),
            _const_spec((mix_width, D)), _const_spec((1, D)), _const_spec((1, D)),
            _const_spec((D, d_ff)), _const_spec((d_ff, D)), _const_spec((1, D)), _const_spec((1, D)),
            _const_spec((ple_dim, D)), _const_spec((D, D)), _const_spec((1, D)), _const_spec((1, D)),
        ],
        out_specs=pl.BlockSpec((1, TM_OUT, D), lambda b, t: (b, t, 0)),
        out_shape=jax.ShapeDtypeStruct((B, S, D), x.dtype),
        compiler_params=cparams(dimension_semantics=("parallel", "parallel")),
        name="hymba_output_stage",
    )(x, o, pooled, p_i, row(attn_g),
      row(g0), row(b0), w_out.astype(BF16), row(ln1_g), row(ln1_b),
      w_up.astype(BF16), w_down.astype(BF16), row(ln2_g), row(ln2_b),
      w_ple.astype(BF16), w_gate.astype(BF16), row(ln3_g), row(ln3_b))
    return out


def kernel(x, p, emb_ln_g, emb_ln_b, w_in, attn_out_g, w_pool, pool_scale, w_out, ln1_g, ln1_b, w_up, w_down, ln2_g, ln2_b, w_ple, w_ple_gate, ln3_g, ln3_b):
    depth = w_in.shape[0]
    assert depth == 1, "the fused output stage re-derives the layer input from x (single layer)"
    alpha = float((2 * depth) ** 0.25)
    return _layer(x, p[0], emb_ln_g, emb_ln_b, w_in[0], attn_out_g[0], w_pool[0], pool_scale[0],
                  w_out[0], ln1_g[0], ln1_b[0], w_up[0], w_down[0], ln2_g[0], ln2_b[0],
                  w_ple[0], w_ple_gate[0], ln3_g[0], ln3_b[0], alpha)
```

```python
import functools
import math

import jax
import jax.numpy as jnp
from jax import lax
from jax.experimental import pallas as pl
from jax.experimental.pallas import tpu as pltpu

HEAD_DIM = 64
POOL_WINDOWS = (2, 4, 8, 16)
LN_EPS = 1e-5
RMS_EPS = 1e-6
CARRY_DONE = 104.0
LOG2_E = 1.4426950408889634
MASKED = 1e30

LANES = 128
MXU_DIM = 256
MAX_POOL_WINDOW = max(POOL_WINDOWS)
VMEM_LIMIT_BYTES = 56 * 1024 * 1024

TM_IN = 2048
ATT_LANES = MXU_DIM
SUB_Q = ATT_LANES // (ATT_LANES // HEAD_DIM)
WIN = MXU_DIM
SUBS_PER_STEP = 2
PIPE_DEPTH = 5
PIPE_UNROLL = 5
TM_OUT = 512
IN_SUB = MXU_DIM // 2
ROW_SUB = MXU_DIM
FF_CHUNK = 1024

BF16 = jnp.bfloat16
F32 = jnp.float32


def _layer_norm(x, g, b):
    mu = jnp.mean(x, axis=-1, keepdims=True)
    xc = x - mu
    var = jnp.mean(xc * xc, axis=-1, keepdims=True)
    return xc * lax.rsqrt(var + LN_EPS) * g + b


def _head_rms_norm(o, g):
    lane = lax.broadcasted_iota(jnp.int32, (1, LANES), 1)
    first = lane < HEAD_DIM
    out = []
    for c0 in range(0, o.shape[1], LANES):
        tile = o[:, c0:c0 + LANES]
        sq = tile * tile
        s_all = jnp.sum(sq, axis=1, keepdims=True)
        s_first = jnp.sum(jnp.where(first, sq, 0.0), axis=1, keepdims=True)
        ms = jnp.where(first, s_first, s_all - s_first) * (1.0 / HEAD_DIM)
        out.append((tile * lax.rsqrt(ms + RMS_EPS) * g[:, c0:c0 + LANES]).astype(BF16))
    return jnp.concatenate(out, axis=1)


def _const_spec(shape):
    zeros = (0,) * len(shape)
    return pl.BlockSpec(shape, lambda *_: zeros, pipeline_mode=pl.Buffered(1))


def _input_kernel(x_ref, g0_ref, b0_ref, w_in_ref, w_pool_ref, pscale_ref,
                  q_ref, k_ref, v_ref, pooled_ref, ubuf_ref, *, attn_width, pool_group):
    t = pl.program_id(1)
    tm = x_ref.shape[1]
    halo = MAX_POOL_WINDOW

    @pl.when(t == 0)
    def _():
        ubuf_ref[0:halo, :] = jnp.zeros((halo, ubuf_ref.shape[1]), F32)

    @pl.when(t > 0)
    def _():
        ubuf_ref[0:halo, :] = ubuf_ref[tm:tm + halo, :]

    starts = list(range(0, tm, IN_SUB))
    xn = [_layer_norm(x_ref[0, r0:r0 + IN_SUB, :], g0_ref[...], b0_ref[...]) for r0 in starts]
    projs = [jnp.dot(a.astype(BF16), w_in_ref[...], preferred_element_type=F32) for a in xn]

    a = attn_width
    for r0, proj in zip(starts, projs):
        rows = slice(r0, r0 + IN_SUB)
        q_ref[0, rows, :] = (proj[:, :a] * (1.0 / math.sqrt(HEAD_DIM))).astype(BF16)
        k_ref[0, rows, :] = proj[:, a:2 * a].astype(BF16)
        v_ref[0, rows, :] = proj[:, 2 * a:3 * a].astype(BF16)
        u = proj[:, 3 * a:]
        ubuf_ref[halo + r0:halo + r0 + IN_SUB, :] = u

        pos = t * tm + r0 + lax.broadcasted_iota(jnp.int32, (IN_SUB, 1), 0)
        for g, w in enumerate(POOL_WINDOWS):
            lo, hi = g * pool_group, (g + 1) * pool_group
            win = u[:, lo:hi]
            for i in range(1, w):
                win = win + ubuf_ref[halo + r0 - i:halo + r0 - i + IN_SUB, lo:hi]
            count = jnp.minimum(pos + 1, w).astype(F32)
            d = win / count - u[:, lo:hi]
            y = jnp.dot(d.astype(BF16), w_pool_ref[g], preferred_element_type=F32)
            pooled_ref[0, rows, lo:hi] = (y * pscale_ref[:, lo:hi]).astype(BF16)


def _attn_kernel(q_ref, k_ref, v_ref, tri_ref, bd_ref, o_ref, osum_ref, carry_ref,
                 t_a, t_b, t_c, sp_buf, cs_buf, w_buf):
    s_len, lg = q_ref.shape[1], q_ref.shape[2]
    nh = lg // HEAD_DIM
    n_sub = s_len // SUB_Q
    n_steps = n_sub // SUBS_PER_STEP
    past = WIN - SUB_Q
    head_subs = past // SUB_Q + 1
    head_steps = -(-head_subs // SUBS_PER_STEP)
    first_sub = head_steps * SUBS_PER_STEP
    nt = (((1,), (1,)), ((), ()))
    tn = (((0,), (0,)), ((), ()))

    row_w = lax.broadcasted_iota(jnp.int32, (WIN, 1), 0)
    lane = lax.broadcasted_iota(jnp.int32, (1, lg), 1)
    lane_q = lane % SUB_Q
    lane_head = lane // HEAD_DIM
    diag_valid = lax.broadcasted_iota(jnp.int32, (SUB_Q, 1), 0) < lane_q
    head_masks = [lane_head == h for h in range(nh)]

    def sub_rows(sl):
        return pl.ds(pl.multiple_of(sl * SUB_Q, SUB_Q), SUB_Q)

    def load_qbd(sl):
        q_s = q_ref[0, sub_rows(sl), :]
        return jnp.concatenate([q_s] * nh, axis=0) * bd_ref[...]

    def softplus(t):
        return jnp.maximum(t, 0.0) + jnp.log(1.0 + jnp.exp2(jnp.abs(t) * (-LOG2_E)))

    def head_blocks(res):
        out = res[0:SUB_Q]
        for h in range(1, nh):
            out = jnp.where(head_masks[h], res[h * SUB_Q:(h + 1) * SUB_Q], out)
        return out

    def store_out(sl, o):
        o_ref[0, sub_rows(sl), :] = o.astype(BF16)

    def windows(subs):
        wss = [pl.multiple_of(jnp.maximum(hi - WIN, 0), SUB_Q) for hi, _, _, _ in subs]
        zs = [lax.dot_general(k_ref[0, pl.ds(ws, WIN), :], qbd_t, nt, preferred_element_type=F32)
              for ws, (_, _, qbd_t, _) in zip(wss, subs)]
        ts = [jnp.where((ws + row_w) < jnp.minimum(q0 + lane_q, hi), z, -MASKED)
              for ws, z, (hi, q0, _, _) in zip(wss, zs, subs)]
        csums = [jnp.dot(tri_ref[...], softplus(t).astype(BF16), preferred_element_type=F32) for t in ts]
        ws_bf = [jnp.exp(t - csum - carry).astype(BF16)
                 for t, csum, (_, _, _, carry) in zip(ts, csums, subs)]
        ress = [lax.dot_general(w, v_ref[0, pl.ds(ws, WIN), :], tn, preferred_element_type=F32)
                for w, ws in zip(ws_bf, wss)]
        return [(head_blocks(res), carry + csum[0:1, :], ws)
                for res, csum, ws, (_, _, _, carry) in zip(ress, csums, wss, subs)]

    zero_carry = jnp.zeros((1, lg), F32)
    head_sls = list(range(first_sub))
    outs = windows([((sl + 1) * SUB_Q, sl * SUB_Q, load_qbd(sl), zero_carry) for sl in head_sls])
    for sl, (o, _, _) in zip(head_sls, outs):
        store_out(sl, o)

    def step_subs(step):
        return [step * SUBS_PER_STEP + u for u in range(SUBS_PER_STEP)]

    def window_start(sl):
        return pl.multiple_of(sl * SUB_Q - past, SUB_Q)

    def stage_qk_dots(step):
        return [lax.dot_general(k_ref[0, pl.ds(window_start(sl), WIN), :], load_qbd(sl), nt,
                                preferred_element_type=F32) for sl in step_subs(step)]

    def stage_qk_store(zs):
        for u, z in enumerate(zs):
            t_a[u] = jnp.concatenate([z[:past], jnp.where(diag_valid, z[past:], -MASKED)], axis=0)

    def stage_softplus():
        for u in range(SUBS_PER_STEP):
            t = t_a[u]
            sp_buf[u] = softplus(t).astype(BF16)
            t_b[u] = t

    def stage_csum_dots():
        return [jnp.dot(tri_ref[...], sp_buf[u], preferred_element_type=F32) for u in range(SUBS_PER_STEP)]

    def stage_csum_store(csums):
        for u, cs in enumerate(csums):
            cs_buf[u] = cs
            t_c[u] = t_b[u]

    def stage_weights(step):
        for u, sl in enumerate(step_subs(step)):
            cs = cs_buf[u]
            w_buf[u] = jnp.exp(t_c[u] - cs).astype(BF16).T
            carry_ref[pl.ds(sl, 1), :] = cs[0:1, :]

    def stage_pv(step):
        for u, sl in enumerate(step_subs(step)):
            res = jnp.dot(w_buf[u], v_ref[0, pl.ds(window_start(sl), WIN), :],
                          preferred_element_type=F32)
            o = head_blocks(res)
            osum_ref[sub_rows(sl), :] = o
            store_out(sl, o)

    def iteration(i, live):
        csums = stage_csum_dots() if live(i - 2) else None
        if live(i - 4):
            stage_pv(i - 4)
        zs = stage_qk_dots(i) if live(i) else None
        if live(i - 3):
            stage_weights(i - 3)
        if csums is not None:
            stage_csum_store(csums)
        if live(i - 1):
            stage_softplus()
        if zs is not None:
            stage_qk_store(zs)

    on_grid = lambda step: head_steps <= step < n_steps
    fill_end = head_steps + PIPE_DEPTH - 1
    trips = (n_steps - fill_end) // PIPE_UNROLL
    drain_start = fill_end + trips * PIPE_UNROLL
    for i in range(head_steps, fill_end):
        iteration(i, on_grid)

    def trip(j, _):
        for r in range(PIPE_UNROLL):
            iteration(fill_end + j * PIPE_UNROLL + r, lambda step: True)
        return 0

    lax.fori_loop(0, trips, trip, 0)
    for i in range(drain_start, n_steps + PIPE_DEPTH - 1):
        iteration(i, on_grid)

    @pl.when(jnp.min(carry_ref[first_sub:, :]) < CARRY_DONE)
    def _():
        def fix(sl, _):
            q0 = sl * SUB_Q
            qbd_t = load_qbd(sl)
            carry = carry_ref[pl.ds(sl, 1), :]

            def cond(st):
                return (st[0] > 0) & (st[1] < CARRY_DONE)

            def body(st):
                (o_new, carry_new, ws_new), = windows([(st[0], q0, qbd_t, st[2])])
                return ws_new, jnp.min(carry_new), carry_new, st[3] + o_new

            st = lax.while_loop(cond, body, (q0 - past, jnp.min(carry), carry, osum_ref[sub_rows(sl), :]))
            store_out(sl, st[3])
            return 0

        lax.fori_loop(first_sub, n_sub, fix, 0)


def _output_kernel(x_ref, o_ref, pooled_ref, p_ref, ga_ref,
                   g0_ref, b0_ref, w_out_ref, g1_ref, b1_ref,
                   w_up_ref, w_down_ref, g2_ref, b2_ref,
                   w_ple_ref, w_gate_ref, g3_ref, b3_ref,
                   out_ref, *, alpha):
    tm = x_ref.shape[1]
    subs = [pl.ds(r0, ROW_SUB) for r0 in range(0, tm, ROW_SUB)]
    dot = functools.partial(jnp.dot, preferred_element_type=F32)

    xn = [_layer_norm(x_ref[0, rows, :], g0_ref[...], b0_ref[...]) for rows in subs]
    heads = [_head_rms_norm(o_ref[0, rows, :].astype(F32), ga_ref[...]) for rows in subs]
    mixed = [dot(jnp.concatenate([a, pooled_ref[0, rows, :]], axis=-1), w_out_ref[...])
             for a, rows in zip(heads, subs)]
    x1 = [_layer_norm(alpha * a + m, g1_ref[...], b1_ref[...]) for a, m in zip(xn, mixed)]

    x1b = [a.astype(BF16) for a in x1]
    d_ff = w_up_ref.shape[1]
    y = [None] * len(subs)
    for c0 in range(0, d_ff, FF_CHUNK):
        h = [dot(a, w_up_ref[:, c0:c0 + FF_CHUNK]) for a in x1b]
        h = [jnp.square(jnp.maximum(a, 0.0)).astype(BF16) for a in h]
        part = [dot(a, w_down_ref[c0:c0 + FF_CHUNK, :]) for a in h]
        y = [b if a is None else a + b for a, b in zip(y, part)]
    x2 = [_layer_norm(alpha * a + b, g2_ref[...], b2_ref[...]) for a, b in zip(x1, y)]

    gate = [jax.nn.sigmoid(dot(a.astype(BF16), w_gate_ref[...])) for a in x2]
    ple = [dot(p_ref[0, rows, :].astype(BF16), w_ple_ref[...]) * gt for rows, gt in zip(subs, gate)]
    for rows, a, b in zip(subs, x2, ple):
        out_ref[0, rows, :] = _layer_norm(alpha * a + b, g3_ref[...], b3_ref[...])


def _layer(x, p_i, g0, b0, w_in, attn_g, w_pool, pool_scale, w_out, ln1_g, ln1_b,
           w_up, w_down, ln2_g, ln2_b, w_ple, w_gate, ln3_g, ln3_b, alpha):
    B, S, D = x.shape
    n_groups, pool_group, _ = w_pool.shape
    pool_width = n_groups * pool_group
    attn_width = (w_in.shape[1] - pool_width) // 3
    mix_width = attn_width + pool_width
    d_ff = w_up.shape[1]
    ple_dim = p_i.shape[-1]
    assert S % TM_IN == 0 and S % TM_OUT == 0 and S % (SUB_Q * SUBS_PER_STEP) == 0
    assert attn_width % ATT_LANES == 0 and WIN % SUB_Q == 0 and S >= 2 * WIN
    assert d_ff % FF_CHUNK == 0 and IN_SUB >= MAX_POOL_WINDOW and TM_IN % IN_SUB == 0 and TM_OUT % ROW_SUB == 0

    row = lambda v: v.reshape(1, -1).astype(F32)
    cparams = functools.partial(pltpu.CompilerParams, vmem_limit_bytes=VMEM_LIMIT_BYTES)

    act = lambda width: jax.ShapeDtypeStruct((B, S, width), BF16)
    q, k, v, pooled = pl.pallas_call(
        functools.partial(_input_kernel, attn_width=attn_width, pool_group=pool_group),
        grid=(B, S // TM_IN),
        in_specs=[
            pl.BlockSpec((1, TM_IN, D), lambda b, t: (b, t, 0)),
            _const_spec((1, D)), _const_spec((1, D)),
            _const_spec(w_in.shape), _const_spec(w_pool.shape), _const_spec((1, pool_width)),
        ],
        out_specs=[
            pl.BlockSpec((1, TM_IN, attn_width), lambda b, t: (b, t, 0)),
            pl.BlockSpec((1, TM_IN, attn_width), lambda b, t: (b, t, 0)),
            pl.BlockSpec((1, TM_IN, attn_width), lambda b, t: (b, t, 0)),
            pl.BlockSpec((1, TM_IN, pool_width), lambda b, t: (b, t, 0)),
        ],
        out_shape=[act(attn_width), act(attn_width), act(attn_width), act(pool_width)],
        scratch_shapes=[pltpu.VMEM((TM_IN + MAX_POOL_WINDOW, pool_width), F32)],
        compiler_params=cparams(dimension_semantics=("parallel", "arbitrary")),
        name="hymba_input_stage",
    )(x, row(g0), row(b0), w_in.astype(BF16), w_pool.astype(BF16), row(pool_scale))

    n_groups_attn = attn_width // ATT_LANES
    suffix_ones = jnp.triu(jnp.ones((WIN, WIN), BF16))
    head_of = jnp.arange(ATT_LANES, dtype=jnp.int32)
    block_diag = (head_of[:, None] // SUB_Q == head_of[None, :] // HEAD_DIM).astype(BF16)
    seq_spec = pl.BlockSpec((1, S, ATT_LANES), lambda b, hg: (b, 0, hg))
    o = pl.pallas_call(
        _attn_kernel,
        grid=(B, n_groups_attn),
        in_specs=[seq_spec, seq_spec, seq_spec, _const_spec((WIN, WIN)), _const_spec((ATT_LANES, ATT_LANES))],
        out_specs=seq_spec,
        out_shape=act(attn_width),
        scratch_shapes=[pltpu.VMEM((S, ATT_LANES), F32), pltpu.VMEM((S // SUB_Q, ATT_LANES), F32),
                        pltpu.VMEM((SUBS_PER_STEP, WIN, ATT_LANES), F32),
                        pltpu.VMEM((SUBS_PER_STEP, WIN, ATT_LANES), F32),
                        pltpu.VMEM((SUBS_PER_STEP, WIN, ATT_LANES), F32),
                        pltpu.VMEM((SUBS_PER_STEP, WIN, ATT_LANES), BF16),
                        pltpu.VMEM((SUBS_PER_STEP, WIN, ATT_LANES), F32),
                        pltpu.VMEM((SUBS_PER_STEP, WIN, ATT_LANES), BF16)],
        compiler_params=cparams(dimension_semantics=("parallel", "parallel")),
        name="hymba_stickbreak_attention",
    )(q, k, v, suffix_ones, block_diag)

    out = pl.pallas_call(
        functools.partial(_output_kernel, alpha=alpha),
        grid=(B, S // TM_OUT),
        in_specs=[
            pl.BlockSpec((1, TM_OUT, D), lambda b, t: (b, t, 0)),
            pl.BlockSpec((1, TM_OUT, attn_width), lambda b, t: (b, t, 0)),
            pl.BlockSpec((1, TM_OUT, pool_width), lambda b, t: (b, t, 0)),
            pl.BlockSpec((1, TM_OUT, ple_dim), lambda b, t: (b, t, 0)),
            _const_spec((1, attn_width)),
            _const_spec((1, D)), _const_spec((1, D)),
            _const_spec((mix_width, D)), _const_spec((1, D)), _const_spec((1, D)),
            _const_spec((D, d_ff)), _const_spec((d_ff, D)), _const_spec((1, D)), _const_spec((1, D)),
            _const_spec((ple_dim, D)), _const_spec((D, D)), _const_spec((1, D)), _const_spec((1, D)),
        ],
        out_specs=pl.BlockSpec((1, TM_OUT, D), lambda b, t: (b, t, 0)),
        out_shape=jax.ShapeDtypeStruct((B, S, D), x.dtype),
        compiler_params=cparams(dimension_semantics=("parallel", "parallel")),
        name="hymba_output_stage",
    )(x, o, pooled, p_i, row(attn_g),
      row(g0), row(b0), w_out.astype(BF16), row(ln1_g), row(ln1_b),
      w_up.astype(BF16), w_down.astype(BF16), row(ln2_g), row(ln2_b),
      w_ple.astype(BF16), w_gate.astype(BF16), row(ln3_g), row(ln3_b))
    return out


def kernel(x, p, emb_ln_g, emb_ln_b, w_in, attn_out_g, w_pool, pool_scale, w_out, ln1_g, ln1_b, w_up, w_down, ln2_g, ln2_b, w_ple, w_ple_gate, ln3_g, ln3_b):
    depth = w_in.shape[0]
    assert depth == 1, "the fused output stage re-derives the layer input from x (single layer)"
    alpha = float((2 * depth) ** 0.25)
    return _layer(x, p[0], emb_ln_g, emb_ln_b, w_in[0], attn_out_g[0], w_pool[0], pool_scale[0],
                  w_out[0], ln1_g[0], ln1_b[0], w_up[0], w_down[0], ln2_g[0], ln2_b[0],
                  w_ple[0], w_ple_gate[0], ln3_g[0], ln3_b[0], alpha)
```
